```python
import math
import jax, jax.numpy as jnp
from jax import lax
import numpy as np


D_MODEL = 2048
BATCH = 16
SEQ = 256
DEPTH = 1
DEC_BATCH = 4
DEC_SEQ = 4096
PAST_LEN = 512

GRID_W = 64
N_HEADS = 8
HEAD_DIM = 128
V_HEAD_DIM = 2 * HEAD_DIM
ATTN_W = N_HEADS * V_HEAD_DIM
D_CONV = 1024
CONV_WIDTH = 3
N_EXPERTS = 64
TOP_K = 8
N_GROUPS = 8
TOPK_GROUPS = 4
D_EXPERT = 512
D_SHARED = 512
ROUTED_SCALE = 2.5
ROPE_THETA = 10000.0
Q_BLOCK = 128
EPS = 1e-6
PROJ_SPLITS = (2 * N_HEADS * HEAD_DIM, 2 * N_HEADS * HEAD_DIM, N_HEADS * V_HEAD_DIM,
               D_CONV, D_CONV, D_CONV, D_MODEL, D_MODEL)
D_PROJ = sum(PROJ_SPLITS)

kernel_name = "hybrid_diff_attn_shortconv_moe_step"


def _rmsnorm(x, g):
    xf = x.astype(jnp.float32)
    xf = xf * lax.rsqrt(jnp.mean(xf * xf, axis=-1, keepdims=True) + EPS)
    return xf.astype(x.dtype) * g


def _axial_rope_tables(n_tokens):
    rows = n_tokens // GRID_W
    row = jnp.repeat(jnp.arange(rows, dtype=jnp.float32), GRID_W)
    col = jnp.tile(jnp.arange(GRID_W, dtype=jnp.float32), rows)
    n_freq = HEAD_DIM // 4
    freqs = ROPE_THETA ** (-jnp.arange(n_freq, dtype=jnp.float32) / n_freq)
    ang_r = row[:, None] * freqs
    ang_c = col[:, None] * freqs
    ang = jnp.concatenate([ang_r, ang_r, ang_c, ang_c], axis=-1)
    return jnp.cos(ang), jnp.sin(ang)


def _apply_rope(x, cos, sin):
    xr = x.reshape(x.shape[:-1] + (2, 2, HEAD_DIM // 4))
    rot = jnp.stack([-xr[..., 1, :], xr[..., 0, :]], axis=-2).reshape(x.shape)
    cos = cos[None, :, None, None, :].astype(x.dtype)
    sin = sin[None, :, None, None, :].astype(x.dtype)
    return x * cos + rot * sin


def _diff_attention(q, k, v, lam):
    b, nq = q.shape[:2]
    nb = nq // Q_BLOCK
    qb = jnp.moveaxis(q.reshape(b, nb, Q_BLOCK, 2, N_HEADS, HEAD_DIM), 1, 0)
    scale = HEAD_DIM ** -0.5

    def block(qblk):
        s = jnp.einsum('bqjhd,bkjhd->bjhqk', qblk, k).astype(jnp.float32) * scale
        p = jax.nn.softmax(s, axis=-1)
        a = p[:, 0] - lam * p[:, 1]
        return jnp.einsum('bhqk,bkhe->bqhe', a.astype(v.dtype), v)

    out = lax.map(block, qb)
    return jnp.moveaxis(out, 0, 1).reshape(b, nq, N_HEADS, V_HEAD_DIM)


def _short_conv(u, w):
    pad = CONV_WIDTH // 2
    n = u.shape[1]
    up = jnp.pad(u, ((0, 0), (pad, pad), (0, 0)))
    return sum(up[:, i:i + n] * w[i] for i in range(CONV_WIDTH))


def _moe(h, p):
    b, n, d = h.shape
    t = h.reshape(b * n, d)
    s = jax.nn.sigmoid((t @ p['w_router']).astype(jnp.float32))
    biased = s + p['router_bias'].astype(jnp.float32)
    per_grp = N_EXPERTS // N_GROUPS
    grp_score = lax.top_k(biased.reshape(-1, N_GROUPS, per_grp), 2)[0].sum(-1)
    _, grp_idx = lax.top_k(grp_score, TOPK_GROUPS)
    grp_mask = jax.nn.one_hot(grp_idx, N_GROUPS, dtype=jnp.float32).sum(1)
    expert_mask = jnp.repeat(grp_mask, per_grp, axis=1)
    masked = jnp.where(expert_mask > 0, biased, -jnp.inf)
    _, idx = lax.top_k(masked, TOP_K)
    w = jnp.take_along_axis(s, idx, axis=1)
    w = w / jnp.sum(w, axis=-1, keepdims=True) * ROUTED_SCALE
    combine = jnp.sum(jax.nn.one_hot(idx, N_EXPERTS, dtype=jnp.float32) * w[..., None], axis=1)

    def expert(acc, xs):
        wg, wu, wd, ce = xs
        hid = jax.nn.silu(t @ wg) * (t @ wu)
        return acc + ce[:, None] * (hid @ wd).astype(jnp.float32), None

    routed, _ = lax.scan(expert, jnp.zeros((t.shape[0], d), jnp.float32),
                         (p['w_exp_gate'], p['w_exp_up'], p['w_exp_down'], combine.T))
    shared = (jax.nn.silu(t @ p['w_sh_gate']) * (t @ p['w_sh_up'])) @ p['w_sh_down']
    return (routed.astype(h.dtype) + shared).reshape(b, n, d)


def _layer(x, mod, p, layer_idx, rope=None, ctx_kv=None):
    shift1, scale1, gate1, shift2, scale2, gate2 = jnp.split(mod, 6, axis=-1)
    b, n, _ = x.shape
    h = _rmsnorm(x, p['g_pre_mix']) * (1 + scale1) + shift1
    q, k, v, cb, cc, cx, ga, gc = jnp.split(h @ p['w_in'], np.cumsum(PROJ_SPLITS)[:-1].tolist(), axis=-1)
    q = q.reshape(b, n, 2, N_HEADS, HEAD_DIM)
    k = k.reshape(b, n, 2, N_HEADS, HEAD_DIM)
    v = v.reshape(b, n, N_HEADS, V_HEAD_DIM)
    lam_init = 0.8 - 0.6 * math.exp(-0.3 * layer_idx)
    lam = (jnp.exp(jnp.sum(p['lambda_q1'] * p['lambda_k1']).astype(jnp.float32))
           - jnp.exp(jnp.sum(p['lambda_q2'] * p['lambda_k2']).astype(jnp.float32)) + lam_init)
    if ctx_kv is None:
        keys, vals = k, v
        new_kv = (k, v)
    else:
        cos, sin = rope
        q = _apply_rope(q, cos, sin)
        keys = jnp.concatenate([_apply_rope(k, cos, sin), ctx_kv[0]], axis=1)
        vals = jnp.concatenate([v, ctx_kv[1]], axis=1)
        new_kv = None
    attn = _diff_attention(q, keys, vals, lam)
    attn = (_rmsnorm(attn, p['g_subln']) * (1 - lam_init)).reshape(b, n, ATTN_W)
    conv = cb * _short_conv(cc * cx, p['conv_w'])
    merged = (jax.nn.sigmoid(ga) * (attn @ p['w_attn_out'])
              + jax.nn.sigmoid(gc) * (conv @ p['w_conv_out']))
    x = x + gate1 * _rmsnorm(merged @ p['w_o'], p['g_post_mix'])
    h2 = _rmsnorm(x, p['g_pre_ffn']) * (1 + scale2) + shift2
    x = x + gate2 * _rmsnorm(_moe(h2, p), p['g_post_ffn'])
    return x, new_kv


def setup_inputs(seed: int = 0) -> dict:
    key = jax.random.key(seed)
    ks = jax.random.split(key, 32)

    def nrm(i, shape, scale):
        return jax.random.normal(ks[i], shape, jnp.float32) * scale

    L, D = DEPTH, D_MODEL
    return {
        'x_prompt': nrm(0, (BATCH, SEQ, D), 1.0),
        'x_sample': nrm(1, (DEC_BATCH, DEC_SEQ, D), 1.0),
        'cache_k': nrm(2, (DEC_BATCH, DEPTH, PAST_LEN, 2, N_HEADS, HEAD_DIM), 1.0),
        'cache_v': nrm(3, (DEC_BATCH, DEPTH, PAST_LEN, N_HEADS, V_HEAD_DIM), 1.0),
        'c': nrm(4, (DEC_BATCH, D), 1.0),
        'c_ctx': nrm(5, (D,), 1.0),
        'w_ada': nrm(6, (L, D, 6 * D), 0.5 * D ** -0.5),
        'b_ada': nrm(7, (L, 6 * D), 0.01),
        'g_pre_mix': 1.0 + nrm(8, (L, D), 0.02),
        'g_post_mix': 1.0 + nrm(9, (L, D), 0.02),
        'w_in': nrm(10, (L, D, D_PROJ), D ** -0.5),
        'lambda_q1': nrm(11, (L, HEAD_DIM), 0.1),
        'lambda_k1': nrm(12, (L, HEAD_DIM), 0.1),
        'lambda_q2': nrm(13, (L, HEAD_DIM), 0.1),
        'lambda_k2': nrm(14, (L, HEAD_DIM), 0.1),
        'g_subln': 1.0 + nrm(15, (L, V_HEAD_DIM), 0.02),
        'conv_w': nrm(16, (L, CONV_WIDTH, D_CONV), CONV_WIDTH ** -0.5),
        'w_attn_out': nrm(17, (L, ATTN_W, D), ATTN_W ** -0.5),
        'w_conv_out': nrm(18, (L, D_CONV, D), D_CONV ** -0.5),
        'w_o': nrm(19, (L, D, D), D ** -0.5),
        'g_pre_ffn': 1.0 + nrm(20, (L, D), 0.02),
        'g_post_ffn': 1.0 + nrm(21, (L, D), 0.02),
        'w_router': nrm(22, (L, D, N_EXPERTS), D ** -0.5),
        'router_bias': nrm(23, (L, N_EXPERTS), 0.01),
        'w_exp_gate': nrm(24, (L, N_EXPERTS, D, D_EXPERT), D ** -0.5),
        'w_exp_up': nrm(25, (L, N_EXPERTS, D, D_EXPERT), D ** -0.5),
        'w_exp_down': nrm(26, (L, N_EXPERTS, D_EXPERT, D), D_EXPERT ** -0.5),
        'w_sh_gate': nrm(27, (L, D, D_SHARED), D ** -0.5),
        'w_sh_up': nrm(28, (L, D, D_SHARED), D ** -0.5),
        'w_sh_down': nrm(29, (L, D_SHARED, D), D_SHARED ** -0.5),
    }


def reference(x_prompt, x_sample, cache_k, cache_v, c, c_ctx, w_ada, b_ada, g_pre_mix, g_post_mix,
              w_in, lambda_q1, lambda_k1, lambda_q2, lambda_k2, g_subln, conv_w, w_attn_out,
              w_conv_out, w_o, g_pre_ffn, g_post_ffn, w_router, router_bias, w_exp_gate, w_exp_up,
              w_exp_down, w_sh_gate, w_sh_up, w_sh_down):
    def params(l):
        return {
            'g_pre_mix': g_pre_mix[l], 'g_post_mix': g_post_mix[l], 'w_in': w_in[l],
            'lambda_q1': lambda_q1[l], 'lambda_k1': lambda_k1[l],
            'lambda_q2': lambda_q2[l], 'lambda_k2': lambda_k2[l],
            'g_subln': g_subln[l], 'conv_w': conv_w[l], 'w_attn_out': w_attn_out[l],
            'w_conv_out': w_conv_out[l], 'w_o': w_o[l], 'g_pre_ffn': g_pre_ffn[l],
            'g_post_ffn': g_post_ffn[l], 'w_router': w_router[l], 'router_bias': router_bias[l],
            'w_exp_gate': w_exp_gate[l], 'w_exp_up': w_exp_up[l], 'w_exp_down': w_exp_down[l],
            'w_sh_gate': w_sh_gate[l], 'w_sh_up': w_sh_up[l], 'w_sh_down': w_sh_down[l],
        }

    xp = x_prompt
    ks, vs = [], []
    for l in range(DEPTH):
        mod_ctx = (jax.nn.silu(c_ctx)[None, :] @ w_ada[l] + b_ada[l])[:, None, :]
        xp, (k_ctx, v_ctx) = _layer(xp, mod_ctx, params(l), l)
        ks.append(k_ctx)
        vs.append(v_ctx)
    new_cache_k = jnp.stack(ks, axis=1)
    new_cache_v = jnp.stack(vs, axis=1)

    rope = _axial_rope_tables(x_sample.shape[1])
    xs = x_sample
    for l in range(DEPTH):
        mod_lat = (jax.nn.silu(c) @ w_ada[l] + b_ada[l])[:, None, :]
        xs, _ = _layer(xs, mod_lat, params(l), l, rope=rope,
                       ctx_kv=(cache_k[:, l], cache_v[:, l]))

    return (xp, xs, new_cache_k, new_cache_v)
```

```python
import functools
import math

import jax
import jax.numpy as jnp
from jax import lax
from jax.experimental import pallas as pl
from jax.experimental.pallas import tpu as pltpu

F32 = jnp.float32
BF16 = jnp.bfloat16

EPS = 1e-6
N_HEADS = 8
HEAD_DIM = 128
V_HEAD_DIM = 2 * HEAD_DIM
D_CONV = 1024
N_EXPERTS = 64
TOP_K = 8
N_GROUPS = 8
TOPK_GROUPS = 4
ROUTED_SCALE = 2.5
ROPE_THETA = 10000.0
GRID_W = 64
LAM_INIT = 0.8 - 0.6 * math.exp(-0.3 * 0)

LANES = 128
HALO_ROWS = 16
VMEM_LIMIT_BYTES = 56 * 1024 * 1024


def _params(*sem):
    return pltpu.CompilerParams(dimension_semantics=sem, vmem_limit_bytes=VMEM_LIMIT_BYTES)


def _rms(x):
    return x * lax.rsqrt(jnp.mean(x * x, axis=-1, keepdims=True) + EPS)


def _ada_kernel(c_ref, w_ref, b_ref, o_ref):
    c = c_ref[...]
    a = (c * jax.nn.sigmoid(c)).astype(BF16)
    o_ref[...] = jnp.dot(a, w_ref[...].astype(BF16), preferred_element_type=F32) + b_ref[...]


def _ada(c_rows, w_ada, b_ada, tn=1024):
    r, d = c_rows.shape
    n = w_ada.shape[1]
    return pl.pallas_call(
        _ada_kernel,
        out_shape=jax.ShapeDtypeStruct((r, n), F32),
        grid=(n // tn,),
        in_specs=[pl.BlockSpec((r, d), lambda j: (0, 0)),
                  pl.BlockSpec((d, tn), lambda j: (0, j)),
                  pl.BlockSpec((1, tn), lambda j: (0, j))],
        out_specs=pl.BlockSpec((r, tn), lambda j: (0, j)),
        compiler_params=_params("arbitrary"),
        name="ada",
    )(c_rows, w_ada, b_ada)


def _proj_kernel(*refs, rope, n_rope_tiles, sig_start):
    if rope:
        x_ref, mod_ref, g_ref, w_ref, cos_ref, sin_ref, o_ref, h_scr = refs
    else:
        x_ref, mod_ref, g_ref, w_ref, o_ref, h_scr = refs
    j = pl.program_id(1)

    @pl.when(j == 0)
    def _():
        shift = mod_ref[0, 0:1, :]
        scale = mod_ref[0, 1:2, :]
        h = (_rms(x_ref[...]) * g_ref[...]) * (1 + scale) + shift
        h_scr[...] = h.astype(BF16)

    acc = jnp.dot(h_scr[...], w_ref[...], preferred_element_type=F32)
    tn = acc.shape[1]

    if rope:
        @pl.when(j < n_rope_tiles)
        def _():
            cos = cos_ref[...]
            sin = sin_ref[...]
            lane = lax.broadcasted_iota(jnp.int32, cos.shape, 1)
            first = (lane % (HEAD_DIM // 2)) < (HEAD_DIM // 4)
            for s in range(tn // HEAD_DIM):
                xs = acc[:, s * HEAD_DIM:(s + 1) * HEAD_DIM]
                partner = jnp.where(first,
                                    pltpu.roll(xs, HEAD_DIM - HEAD_DIM // 4, axis=1),
                                    pltpu.roll(xs, HEAD_DIM // 4, axis=1))
                o_ref[:, s * HEAD_DIM:(s + 1) * HEAD_DIM] = (xs * cos + partner * sin).astype(o_ref.dtype)
        plain_lo = n_rope_tiles
    else:
        plain_lo = 0

    @pl.when((j >= plain_lo) & (j < sig_start))
    def _():
        o_ref[...] = acc.astype(o_ref.dtype)

    @pl.when(j >= sig_start)
    def _():
        o_ref[...] = jax.nn.sigmoid(acc).astype(o_ref.dtype)


def _proj(x, mods3, g, w, *, mod_row, rope_tabs, out_dtype, gate_col, rope_cols, tm, tn=512):
    t, d = x.shape
    n = w.shape[1]
    rope = rope_tabs is not None
    in_specs = [pl.BlockSpec((tm, d), lambda i, j: (i, 0)),
                pl.BlockSpec((1,) + mods3.shape[1:], lambda i, j: (mod_row(i), 0, 0)),
                pl.BlockSpec((1, d), lambda i, j: (0, 0)),
                pl.BlockSpec((d, tn), lambda i, j: (0, j))]
    args = [x, mods3, g, w]
    if rope:
        cos, sin = rope_tabs
        nblk = cos.shape[0] // tm
        in_specs += [pl.BlockSpec((tm, HEAD_DIM), lambda i, j: (i % nblk, 0)),
                     pl.BlockSpec((tm, HEAD_DIM), lambda i, j: (i % nblk, 0))]
        args += [cos, sin]
    kern = functools.partial(_proj_kernel, rope=rope, n_rope_tiles=rope_cols // tn,
                             sig_start=gate_col // tn)
    return pl.pallas_call(
        kern,
        out_shape=jax.ShapeDtypeStruct((t, n), out_dtype),
        grid=(t // tm, n // tn),
        in_specs=in_specs,
        out_specs=pl.BlockSpec((tm, tn), lambda i, j: (i, j)),
        scratch_shapes=[pltpu.VMEM((tm, d), BF16)],
        compiler_params=_params("parallel", "arbitrary"),
        name="proj_rope" if rope else "proj",
    )(*args)


def _attn_kernel(*refs, has_ctx):
    if has_ctx:
        (lam_ref, gs_ref, q1_ref, q2_ref, k1_ref, k2_ref, v_ref,
         k1c_ref, k2c_ref, vc_ref, o_ref) = refs
    else:
        lam_ref, gs_ref, q1_ref, q2_ref, k1_ref, k2_ref, v_ref, o_ref = refs
    lv = lam_ref[...]
    lam = (jnp.exp(jnp.sum(lv[0:1] * lv[1:2], axis=-1, keepdims=True))
           - jnp.exp(jnp.sum(lv[2:3] * lv[3:4], axis=-1, keepdims=True)) + LAM_INIT)
    c = (HEAD_DIM ** -0.5) * math.log2(math.e)
    nt = (((1,), (1,)), ((), ()))

    def softmax_parts(q_ref, k_ref, kc_ref):
        q = q_ref[0].astype(BF16)
        s = lax.dot_general(q, k_ref[0].astype(BF16), nt, preferred_element_type=F32)
        m = jnp.max(s, axis=-1, keepdims=True)
        if has_ctx:
            sc = lax.dot_general(q, kc_ref[0].astype(BF16), nt, preferred_element_type=F32)
            m = jnp.maximum(m, jnp.max(sc, axis=-1, keepdims=True))
            pc = jnp.exp2((sc - m) * c)
        p = jnp.exp2((s - m) * c)
        l = jnp.sum(p, axis=-1, keepdims=True)
        if has_ctx:
            l = l + jnp.sum(pc, axis=-1, keepdims=True)
            return p, pc, l
        return p, None, l

    p1, pc1, l1 = softmax_parts(q1_ref, k1_ref, k1c_ref if has_ctx else None)
    p2, pc2, l2 = softmax_parts(q2_ref, k2_ref, k2c_ref if has_ctx else None)
    r1 = 1.0 / l1
    r2 = lam / l2
    a = (p1 * r1 - p2 * r2).astype(BF16)
    o = jnp.dot(a, v_ref[0].astype(BF16), preferred_element_type=F32)
    if has_ctx:
        ac = (pc1 * r1 - pc2 * r2).astype(BF16)
        o = o + jnp.dot(ac, vc_ref[0].astype(BF16), preferred_element_type=F32)
    o_ref[0] = ((_rms(o) * gs_ref[...]) * (1 - LAM_INIT)).astype(o_ref.dtype)


def _attention(proj3, lam_vecs, g_subln, ctx, *, tq):
    b, n, _ = proj3.shape
    h = N_HEADS
    has_ctx = ctx is not None
    q_spec = lambda off: pl.BlockSpec((1, tq, HEAD_DIM), lambda bi, hi, qi: (bi, qi, off + hi))
    k_spec = lambda off: pl.BlockSpec((1, n, HEAD_DIM), lambda bi, hi, qi: (bi, 0, off + hi))
    in_specs = [pl.BlockSpec((4, HEAD_DIM), lambda bi, hi, qi: (0, 0)),
                pl.BlockSpec((1, V_HEAD_DIM), lambda bi, hi, qi: (0, 0)),
                q_spec(0), q_spec(h), k_spec(2 * h), k_spec(3 * h),
                pl.BlockSpec((1, n, V_HEAD_DIM), lambda bi, hi, qi: (bi, 0, 2 * h + hi))]
    args = [lam_vecs, g_subln, proj3, proj3, proj3, proj3, proj3]
    if has_ctx:
        ck, cv = ctx
        p = ck.shape[1]
        in_specs += [pl.BlockSpec((1, p, HEAD_DIM), lambda bi, hi, qi: (bi, 0, hi)),
                     pl.BlockSpec((1, p, HEAD_DIM), lambda bi, hi, qi: (bi, 0, h + hi)),
                     pl.BlockSpec((1, p, V_HEAD_DIM), lambda bi, hi, qi: (bi, 0, hi))]
        args += [ck, ck, cv]
    return pl.pallas_call(
        functools.partial(_attn_kernel, has_ctx=has_ctx),
        out_shape=jax.ShapeDtypeStruct((b, n, h * V_HEAD_DIM), BF16),
        grid=(b, h, n // tq),
        in_specs=in_specs,
        out_specs=pl.BlockSpec((1, tq, V_HEAD_DIM), lambda bi, hi, qi: (bi, qi, hi)),
        compiler_params=_params("parallel", "parallel", "arbitrary"),
        name="attn_ctx" if has_ctx else "attn",
    )(*args)


def _merge_kernel(attn_ref, cb_ref, cc_ref, cx_ref, ccp_ref, cxp_ref, ccn_ref, cxn_ref,
                  cw_ref, wa_ref, wc_ref, ga_ref, gc_ref, o_ref, conv_scr, *, seq_len):
    i = pl.program_id(0)
    j = pl.program_id(1)
    tm = cc_ref.shape[0]

    @pl.when(j == 0)
    def _():
        u = cc_ref[...].astype(F32) * cx_ref[...].astype(F32)
        u_before = (ccp_ref[...].astype(F32) * cxp_ref[...].astype(F32))[HALO_ROWS - 1:HALO_ROWS, :]
        u_after = (ccn_ref[...].astype(F32) * cxn_ref[...].astype(F32))[0:1, :]
        row = lax.broadcasted_iota(jnp.int32, (tm, 1), 0)
        pos = (i * tm + row) % seq_len
        u_prev = jnp.where(row == 0, u_before, pltpu.roll(u, 1, axis=0))
        u_prev = jnp.where(pos == 0, 0.0, u_prev)
        u_next = jnp.where(row == tm - 1, u_after, pltpu.roll(u, tm - 1, axis=0))
        u_next = jnp.where(pos == seq_len - 1, 0.0, u_next)
        cw = cw_ref[...]
        conv = u_prev * cw[0:1, :] + u * cw[1:2, :] + u_next * cw[2:3, :]
        conv_scr[...] = (cb_ref[...].astype(F32) * conv).astype(BF16)

    ya = jnp.dot(attn_ref[...], wa_ref[...], preferred_element_type=F32)
    yc = jnp.dot(conv_scr[...], wc_ref[...], preferred_element_type=F32)
    o_ref[...] = (ga_ref[...].astype(F32) * ya + gc_ref[...].astype(F32) * yc).astype(o_ref.dtype)


def _merge(attn, proj, conv_w, w_attn_out, w_conv_out, *, seq_len, cols, tm, tn=1024):
    t, aw = attn.shape
    d = w_attn_out.shape[1]
    cb0, cc0, cx0, ga0, gc0 = (cols[k] for k in ("cb", "cc", "cx", "ga", "gc"))
    hb = tm // HALO_ROWS
    last_halo = t // HALO_ROWS - 1
    cblk = lambda c0: pl.BlockSpec((tm, D_CONV), lambda i, j: (i, c0 // D_CONV))
    halo_prev = lambda c0: pl.BlockSpec(
        (HALO_ROWS, D_CONV), lambda i, j: (jnp.maximum(i * hb - 1, 0), c0 // D_CONV))
    halo_next = lambda c0: pl.BlockSpec(
        (HALO_ROWS, D_CONV), lambda i, j: (jnp.minimum((i + 1) * hb, last_halo), c0 // D_CONV))
    in_specs = [pl.BlockSpec((tm, aw), lambda i, j: (i, 0)),
                cblk(cb0), cblk(cc0), cblk(cx0),
                halo_prev(cc0), halo_prev(cx0), halo_next(cc0), halo_next(cx0),
                pl.BlockSpec(conv_w.shape, lambda i, j: (0, 0)),
                pl.BlockSpec((aw, tn), lambda i, j: (0, j)),
                pl.BlockSpec((D_CONV, tn), lambda i, j: (0, j)),
                pl.BlockSpec((tm, tn), lambda i, j: (i, ga0 // tn + j)),
                pl.BlockSpec((tm, tn), lambda i, j: (i, gc0 // tn + j))]
    return pl.pallas_call(
        functools.partial(_merge_kernel, seq_len=seq_len),
        out_shape=jax.ShapeDtypeStruct((t, d), BF16),
        grid=(t // tm, d // tn),
        in_specs=in_specs,
        out_specs=pl.BlockSpec((tm, tn), lambda i, j: (i, j)),
        scratch_shapes=[pltpu.VMEM((tm, D_CONV), BF16)],
        compiler_params=_params("parallel", "arbitrary"),
        name="merge",
    )(attn, proj, proj, proj, proj, proj, proj, proj, conv_w, w_attn_out, w_conv_out, proj, proj)


def _post_kernel(m_ref, x_ref, mod_ref, gpost_ref, gpre_ref, wo_ref, wr_ref,
                 x1_ref, h2_ref, lg_ref):
    y = jnp.dot(m_ref[...], wo_ref[...], preferred_element_type=F32)
    gate1 = mod_ref[0, 2:3, :]
    shift2 = mod_ref[0, 3:4, :]
    scale2 = mod_ref[0, 4:5, :]
    x1 = x_ref[...] + gate1 * (_rms(y) * gpost_ref[...])
    x1_ref[...] = x1
    h2 = ((_rms(x1) * gpre_ref[...]) * (1 + scale2) + shift2).astype(BF16)
    h2_ref[...] = h2
    lg_ref[...] = jnp.dot(h2, wr_ref[...], preferred_element_type=F32)


def _post(merged, x, mods3, g_post, g_pre, w_o, w_router, *, mod_row, tm):
    t, d = x.shape
    e = w_router.shape[1]
    row = lambda i: (i, 0)
    const = lambda i: (0, 0)
    return pl.pallas_call(
        _post_kernel,
        out_shape=(jax.ShapeDtypeStruct((t, d), F32),
                   jax.ShapeDtypeStruct((t, d), BF16),
                   jax.ShapeDtypeStruct((t, e), F32)),
        grid=(t // tm,),
        in_specs=[pl.BlockSpec((tm, d), row), pl.BlockSpec((tm, d), row),
                  pl.BlockSpec((1,) + mods3.shape[1:], lambda i: (mod_row(i), 0, 0)),
                  pl.BlockSpec((1, d), const), pl.BlockSpec((1, d), const),
                  pl.BlockSpec((d, d), const), pl.BlockSpec((d, e), const)],
        out_specs=(pl.BlockSpec((tm, d), row), pl.BlockSpec((tm, d), row),
                   pl.BlockSpec((tm, e), row)),
        compiler_params=_params("parallel"),
        name="post",
    )(merged, x, mods3, g_post, g_pre, w_o, w_router)


def _route_kernel(lg_ref, bias_ref, o_ref):
    s = jax.nn.sigmoid(lg_ref[...])
    biased = s + bias_ref[...]
    tm, e = s.shape
    per = e // N_GROUPS
    lane = lax.broadcasted_iota(jnp.int32, (tm, e), 1)
    grp = lane // per
    neg = -jnp.inf

    def first_argmax(v):
        m = jnp.max(v, axis=-1, keepdims=True)
        idx = jnp.min(jnp.where(v == m, lane, e), axis=-1, keepdims=True)
        return m, idx

    scores = []
    for g in range(N_GROUPS):
        vg = jnp.where(grp == g, biased, neg)
        m1, i1 = first_argmax(vg)
        m2 = jnp.max(jnp.where(lane == i1, neg, vg), axis=-1, keepdims=True)
        scores.append(m1 + m2)
    allowed = jnp.zeros((tm, e), jnp.bool_)
    for g in range(N_GROUPS):
        beaten_by = jnp.zeros((tm, 1), jnp.int32)
        for g2 in range(N_GROUPS):
            if g2 == g:
                continue
            wins = (scores[g2] > scores[g]) | ((scores[g2] == scores[g]) & (g2 < g))
            beaten_by = beaten_by + wins.astype(jnp.int32)
        allowed = allowed | ((grp == g) & (beaten_by < TOPK_GROUPS))
    masked = jnp.where(allowed, biased, neg)
    chosen = jnp.zeros((tm, e), jnp.bool_)
    for _ in range(TOP_K):
        _, idx = first_argmax(masked)
        hit = lane == idx
        chosen = chosen | hit
        masked = jnp.where(hit, neg, masked)
    w = jnp.where(chosen, s, 0.0)
    o_ref[...] = w / jnp.sum(w, axis=-1, keepdims=True) * ROUTED_SCALE


def _route(logits, bias, *, tm):
    t, e = logits.shape
    return pl.pallas_call(
        _route_kernel,
        out_shape=jax.ShapeDtypeStruct((t, e), F32),
        grid=(t // tm,),
        in_specs=[pl.BlockSpec((tm, e), lambda i: (i, 0)), pl.BlockSpec((1, e), lambda i: (0, 0))],
        out_specs=pl.BlockSpec((tm, e), lambda i: (i, 0)),
        compiler_params=_params("parallel"),
        name="route",
    )(logits, bias)


def _moe_kernel(h_ref, cmb_ref, wg_ref, wu_ref, wd_ref, sg_ref, su_ref, sd_ref,
                x1_ref, mod_ref, gpost_ref, o_ref, acc_scr):
    e = pl.program_id(1)
    h = h_ref[...]

    def ffn(wg, wu, wd):
        a = jnp.dot(h, wg, preferred_element_type=F32)
        b = jnp.dot(h, wu, preferred_element_type=F32)
        hid = (a * jax.nn.sigmoid(a)) * b
        return jnp.dot(hid.astype(BF16), wd, preferred_element_type=F32)

    @pl.when(e == 0)
    def _():
        acc_scr[...] = ffn(sg_ref[...], su_ref[...], sd_ref[...])

    lane = lax.broadcasted_iota(jnp.int32, cmb_ref.shape, 1)
    ce = jnp.sum(jnp.where(lane == e, cmb_ref[...], 0.0), axis=-1, keepdims=True)
    acc_scr[...] += ce * ffn(wg_ref[0], wu_ref[0], wd_ref[0])

    @pl.when(e == pl.num_programs(1) - 1)
    def _():
        gate2 = mod_ref[0, 5:6, :]
        o_ref[...] = x1_ref[...] + gate2 * (_rms(acc_scr[...]) * gpost_ref[...])


def _moe(h2, combine, wg, wu, wd, sg, su, sd, x1, mods3, g_post, *, mod_row, tm):
    t, d = h2.shape
    ne, _, de = wg.shape
    ds = sg.shape[1]
    row = lambda i, e: (i, 0)
    const = lambda i, e: (0, 0)
    return pl.pallas_call(
        _moe_kernel,
        out_shape=jax.ShapeDtypeStruct((t, d), F32),
        grid=(t // tm, ne),
        in_specs=[pl.BlockSpec((tm, d), row), pl.BlockSpec((tm, ne), row),
                  pl.BlockSpec((1, d, de), lambda i, e: (e, 0, 0)),
                  pl.BlockSpec((1, d, de), lambda i, e: (e, 0, 0)),
                  pl.BlockSpec((1, de, d), lambda i, e: (e, 0, 0)),
                  pl.BlockSpec((d, ds), const), pl.BlockSpec((d, ds), const),
                  pl.BlockSpec((ds, d), const),
                  pl.BlockSpec((tm, d), row),
                  pl.BlockSpec((1,) + mods3.shape[1:], lambda i, e: (mod_row(i), 0, 0)),
                  pl.BlockSpec((1, d), const)],
        out_specs=pl.BlockSpec((tm, d), row),
        scratch_shapes=[pltpu.VMEM((tm, d), F32)],
        compiler_params=_params("parallel", "arbitrary"),
        name="moe_dense",
    )(h2, combine, wg, wu, wd, sg, su, sd, x1, mods3, g_post)


def _rope_tables(n_tokens):
    rows = n_tokens // GRID_W
    row = jnp.repeat(jnp.arange(rows, dtype=F32), GRID_W)
    col = jnp.tile(jnp.arange(GRID_W, dtype=F32), rows)
    n_freq = HEAD_DIM // 4
    freqs = ROPE_THETA ** (-jnp.arange(n_freq, dtype=F32) / n_freq)
    ang_r = row[:, None] * freqs
    ang_c = col[:, None] * freqs
    ang = jnp.concatenate([ang_r, ang_r, ang_c, ang_c], axis=-1)
    sign = jnp.where((jnp.arange(HEAD_DIM) % (HEAD_DIM // 2)) < HEAD_DIM // 4, -1.0, 1.0)
    return jnp.cos(ang), jnp.sin(ang) * sign


def _trunk(x3, mods3, mod_row, wts, rope_tabs, ctx, *, proj_dtype, tm, tq):
    b, n, d = x3.shape
    x = x3.reshape(b * n, d)
    qk_cols = 4 * N_HEADS * HEAD_DIM
    cols = {"cb": qk_cols + N_HEADS * V_HEAD_DIM}
    cols["cc"] = cols["cb"] + D_CONV
    cols["cx"] = cols["cc"] + D_CONV
    cols["ga"] = cols["cx"] + D_CONV
    cols["gc"] = cols["ga"] + d
    proj = _proj(x, mods3, wts["g_pre_mix"], wts["w_in"], mod_row=mod_row, rope_tabs=rope_tabs,
                 out_dtype=proj_dtype, gate_col=cols["ga"], rope_cols=qk_cols, tm=tm)
    attn = _attention(proj.reshape(b, n, -1), wts["lam_vecs"], wts["g_subln"], ctx, tq=tq)
    merged = _merge(attn.reshape(b * n, -1), proj, wts["conv_w"], wts["w_attn_out"],
                    wts["w_conv_out"], seq_len=n, cols=cols, tm=tm)
    x1, h2, logits = _post(merged, x, mods3, wts["g_post_mix"], wts["g_pre_ffn"], wts["w_o"],
                           wts["w_router"], mod_row=mod_row, tm=tm)
    combine = _route(logits, wts["router_bias"], tm=tm)
    y = _moe(h2, combine, wts["w_exp_gate"], wts["w_exp_up"], wts["w_exp_down"],
             wts["w_sh_gate"], wts["w_sh_up"], wts["w_sh_down"], x1, mods3, wts["g_post_ffn"],
             mod_row=mod_row, tm=tm)
    return y.reshape(b, n, d), proj


def kernel(x_prompt, x_sample, cache_k, cache_v, c, c_ctx, w_ada, b_ada, g_pre_mix, g_post_mix, w_in, lambda_q1, lambda_k1, lambda_q2, lambda_k2, g_subln, conv_w, w_attn_out, w_conv_out, w_o, g_pre_ffn, g_post_ffn, w_router, router_bias, w_exp_gate, w_exp_up, w_exp_down, w_sh_gate, w_sh_up, w_sh_down):
    bp, sp, d = x_prompt.shape
    bs, ss, _ = x_sample.shape
    past = cache_k.shape[2]
    l = 0
    wts = {
        "g_pre_mix": g_pre_mix[l][None], "g_post_mix": g_post_mix[l][None],
        "g_pre_ffn": g_pre_ffn[l][None], "g_post_ffn": g_post_ffn[l][None],
        "g_subln": g_subln[l][None], "conv_w": conv_w[l],
        "lam_vecs": jnp.stack([lambda_q1[l], lambda_k1[l], lambda_q2[l], lambda_k2[l]]),
        "router_bias": router_bias[l][None],
        "w_in": w_in[l].astype(BF16), "w_attn_out": w_attn_out[l].astype(BF16),
        "w_conv_out": w_conv_out[l].astype(BF16), "w_o": w_o[l].astype(BF16),
        "w_router": w_router[l].astype(BF16),
        "w_exp_gate": w_exp_gate[l].astype(BF16), "w_exp_up": w_exp_up[l].astype(BF16),
        "w_exp_down": w_exp_down[l].astype(BF16),
        "w_sh_gate": w_sh_gate[l].astype(BF16), "w_sh_up": w_sh_up[l].astype(BF16),
        "w_sh_down": w_sh_down[l].astype(BF16),
    }
    pad_rows = 8 - (bs + 1)
    c_rows = jnp.concatenate([c, c_ctx[None], jnp.zeros((pad_rows, d), F32)], axis=0)
    mods3 = _ada(c_rows, w_ada[l], b_ada[l][None]).reshape(8, 6, d)

    tm_s, tm_p = 512, 512
    yp, proj_p = _trunk(x_prompt, mods3, lambda i: bs, wts, None, None,
                        proj_dtype=F32, tm=tm_p, tq=sp)
    hk = 2 * N_HEADS * HEAD_DIM
    new_k = proj_p[:, hk:2 * hk].reshape(bp, 1, sp, 2, N_HEADS, HEAD_DIM)
    new_v = proj_p[:, 2 * hk:2 * hk + N_HEADS * V_HEAD_DIM].reshape(bp, 1, sp, N_HEADS, V_HEAD_DIM)

    ctx = (cache_k[:, l].reshape(bs, past, hk).astype(BF16),
           cache_v[:, l].reshape(bs, past, N_HEADS * V_HEAD_DIM).astype(BF16))
    ys, _ = _trunk(x_sample, mods3, lambda i: (i * tm_s) // ss, wts, _rope_tables(ss), ctx,
                   proj_dtype=BF16, tm=tm_s, tq=256)
    return (yp, ys, new_k, new_v)
```

```python
import functools
import math

import jax
import jax.numpy as jnp
from jax import lax
from jax.experimental import pallas as pl
from jax.experimental.pallas import tpu as pltpu

F32 = jnp.float32
BF16 = jnp.bfloat16
I32 = jnp.int32

EPS = 1e-6
N_HEADS = 8
HEAD_DIM = 128
V_HEAD_DIM = 2 * HEAD_DIM
D_CONV = 1024
N_EXPERTS = 64
TOP_K = 8
N_GROUPS = 8
TOPK_GROUPS = 4
ROUTED_SCALE = 2.5
ROPE_THETA = 10000.0
GRID_W = 64
LAM_INIT = 0.8 - 0.6 * math.exp(-0.3 * 0)

LANES = 128
HALO_ROWS = 16
VMEM_LIMIT_BYTES = 56 * 1024 * 1024
EXPERT_TILE = 256
HI_HALF = -65536


def _params(*sem):
    return pltpu.CompilerParams(dimension_semantics=sem, vmem_limit_bytes=VMEM_LIMIT_BYTES)


def _rms(x):
    return x * lax.rsqrt(jnp.mean(x * x, axis=-1, keepdims=True) + EPS)


def _pack_pairs(x):
    n = x.shape[1] // 2
    bits = lax.bitcast_convert_type(x.astype(BF16).astype(F32), I32)
    return lax.shift_right_logical(bits[:, :n], 16) | (bits[:, n:] & HI_HALF)


def _unpack_pairs(p):
    lo = lax.bitcast_convert_type(lax.shift_left(p, 16), F32)
    hi = lax.bitcast_convert_type(p & HI_HALF, F32)
    return jnp.concatenate([lo, hi], axis=1)


def _ada_kernel(c_ref, w_ref, b_ref, o_ref):
    c = c_ref[...]
    a = (c * jax.nn.sigmoid(c)).astype(BF16)
    o_ref[...] = jnp.dot(a, w_ref[...].astype(BF16), preferred_element_type=F32) + b_ref[...]


def _ada(c_rows, w_ada, b_ada, tn=1024):
    r, d = c_rows.shape
    n = w_ada.shape[1]
    return pl.pallas_call(
        _ada_kernel,
        out_shape=jax.ShapeDtypeStruct((r, n), F32),
        grid=(n // tn,),
        in_specs=[pl.BlockSpec((r, d), lambda j: (0, 0)),
                  pl.BlockSpec((d, tn), lambda j: (0, j)),
                  pl.BlockSpec((1, tn), lambda j: (0, j))],
        out_specs=pl.BlockSpec((r, tn), lambda j: (0, j)),
        compiler_params=_params("arbitrary"),
        name="ada",
    )(c_rows, w_ada, b_ada)


def _proj_kernel(*refs, rope, n_rope_tiles, sig_start):
    if rope:
        x_ref, mod_ref, g_ref, w_ref, cos_ref, sin_ref, o_ref, h_scr = refs
    else:
        x_ref, mod_ref, g_ref, w_ref, o_ref, h_scr = refs
    j = pl.program_id(1)

    @pl.when(j == 0)
    def _():
        shift = mod_ref[0, 0:1, :]
        scale = mod_ref[0, 1:2, :]
        h = (_rms(x_ref[...]) * g_ref[...]) * (1 + scale) + shift
        h_scr[...] = h.astype(BF16)

    acc = jnp.dot(h_scr[...], w_ref[...], preferred_element_type=F32)
    tn = acc.shape[1]

    if rope:
        @pl.when(j < n_rope_tiles)
        def _():
            cos = cos_ref[...]
            sin = sin_ref[...]
            lane = lax.broadcasted_iota(jnp.int32, cos.shape, 1)
            first = (lane % (HEAD_DIM // 2)) < (HEAD_DIM // 4)
            for s in range(tn // HEAD_DIM):
                xs = acc[:, s * HEAD_DIM:(s + 1) * HEAD_DIM]
                partner = jnp.where(first,
                                    pltpu.roll(xs, HEAD_DIM - HEAD_DIM // 4, axis=1),
                                    pltpu.roll(xs, HEAD_DIM // 4, axis=1))
                o_ref[:, s * HEAD_DIM:(s + 1) * HEAD_DIM] = (xs * cos + partner * sin).astype(o_ref.dtype)
        plain_lo = n_rope_tiles
    else:
        plain_lo = 0

    @pl.when((j >= plain_lo) & (j < sig_start))
    def _():
        o_ref[...] = acc.astype(o_ref.dtype)

    @pl.when(j >= sig_start)
    def _():
        o_ref[...] = jax.nn.sigmoid(acc).astype(o_ref.dtype)


def _proj(x, mods3, g, w, *, mod_row, rope_tabs, out_dtype, gate_col, rope_cols, tm, tn=512):
    t, d = x.shape
    n = w.shape[1]
    rope = rope_tabs is not None
    in_specs = [pl.BlockSpec((tm, d), lambda i, j: (i, 0)),
                pl.BlockSpec((1,) + mods3.shape[1:], lambda i, j: (mod_row(i), 0, 0)),
                pl.BlockSpec((1, d), lambda i, j: (0, 0)),
                pl.BlockSpec((d, tn), lambda i, j: (0, j))]
    args = [x, mods3, g, w]
    if rope:
        cos, sin = rope_tabs
        nblk = cos.shape[0] // tm
        in_specs += [pl.BlockSpec((tm, HEAD_DIM), lambda i, j: (i % nblk, 0)),
                     pl.BlockSpec((tm, HEAD_DIM), lambda i, j: (i % nblk, 0))]
        args += [cos, sin]
    kern = functools.partial(_proj_kernel, rope=rope, n_rope_tiles=rope_cols // tn,
                             sig_start=gate_col // tn)
    return pl.pallas_call(
        kern,
        out_shape=jax.ShapeDtypeStruct((t, n), out_dtype),
        grid=(t // tm, n // tn),
        in_specs=in_specs,
        out_specs=pl.BlockSpec((tm, tn), lambda i, j: (i, j)),
        scratch_shapes=[pltpu.VMEM((tm, d), BF16)],
        compiler_params=_params("parallel", "arbitrary"),
        name="proj_rope" if rope else "proj",
    )(*args)


def _attn_kernel(*refs, has_ctx):
    if has_ctx:
        (lam_ref, gs_ref, q1_ref, q2_ref, k1_ref, k2_ref, v_ref,
         k1c_ref, k2c_ref, vc_ref, o_ref) = refs
    else:
        lam_ref, gs_ref, q1_ref, q2_ref, k1_ref, k2_ref, v_ref, o_ref = refs
    lv = lam_ref[...]
    lam = (jnp.exp(jnp.sum(lv[0:1] * lv[1:2], axis=-1, keepdims=True))
           - jnp.exp(jnp.sum(lv[2:3] * lv[3:4], axis=-1, keepdims=True)) + LAM_INIT)
    c = (HEAD_DIM ** -0.5) * math.log2(math.e)
    nt = (((1,), (1,)), ((), ()))

    def softmax_parts(q_ref, k_ref, kc_ref):
        q = q_ref[0].astype(BF16)
        s = lax.dot_general(q, k_ref[0].astype(BF16), nt, preferred_element_type=F32)
        m = jnp.max(s, axis=-1, keepdims=True)
        if has_ctx:
            sc = lax.dot_general(q, kc_ref[0].astype(BF16), nt, preferred_element_type=F32)
            m = jnp.maximum(m, jnp.max(sc, axis=-1, keepdims=True))
            pc = jnp.exp2((sc - m) * c)
        p = jnp.exp2((s - m) * c)
        l = jnp.sum(p, axis=-1, keepdims=True)
        if has_ctx:
            l = l + jnp.sum(pc, axis=-1, keepdims=True)
            return p, pc, l
        return p, None, l

    p1, pc1, l1 = softmax_parts(q1_ref, k1_ref, k1c_ref if has_ctx else None)
    p2, pc2, l2 = softmax_parts(q2_ref, k2_ref, k2c_ref if has_ctx else None)
    r1 = 1.0 / l1
    r2 = lam / l2
    a = (p1 * r1 - p2 * r2).astype(BF16)
    o = jnp.dot(a, v_ref[0].astype(BF16), preferred_element_type=F32)
    if has_ctx:
        ac = (pc1 * r1 - pc2 * r2).astype(BF16)
        o = o + jnp.dot(ac, vc_ref[0].astype(BF16), preferred_element_type=F32)
    o_ref[0] = ((_rms(o) * gs_ref[...]) * (1 - LAM_INIT)).astype(o_ref.dtype)


def _attention(proj3, lam_vecs, g_subln, ctx, *, tq):
    b, n, _ = proj3.shape
    h = N_HEADS
    has_ctx = ctx is not None
    q_spec = lambda off: pl.BlockSpec((1, tq, HEAD_DIM), lambda bi, hi, qi: (bi, qi, off + hi))
    k_spec = lambda off: pl.BlockSpec((1, n, HEAD_DIM), lambda bi, hi, qi: (bi, 0, off + hi))
    in_specs = [pl.BlockSpec((4, HEAD_DIM), lambda bi, hi, qi: (0, 0)),
                pl.BlockSpec((1, V_HEAD_DIM), lambda bi, hi, qi: (0, 0)),
                q_spec(0), q_spec(h), k_spec(2 * h), k_spec(3 * h),
                pl.BlockSpec((1, n, V_HEAD_DIM), lambda bi, hi, qi: (bi, 0, 2 * h + hi))]
    args = [lam_vecs, g_subln, proj3, proj3, proj3, proj3, proj3]
    if has_ctx:
        ck, cv = ctx
        p = ck.shape[1]
        in_specs += [pl.BlockSpec((1, p, HEAD_DIM), lambda bi, hi, qi: (bi, 0, hi)),
                     pl.BlockSpec((1, p, HEAD_DIM), lambda bi, hi, qi: (bi, 0, h + hi)),
                     pl.BlockSpec((1, p, V_HEAD_DIM), lambda bi, hi, qi: (bi, 0, hi))]
        args += [ck, ck, cv]
    return pl.pallas_call(
        functools.partial(_attn_kernel, has_ctx=has_ctx),
        out_shape=jax.ShapeDtypeStruct((b, n, h * V_HEAD_DIM), BF16),
        grid=(b, h, n // tq),
        in_specs=in_specs,
        out_specs=pl.BlockSpec((1, tq, V_HEAD_DIM), lambda bi, hi, qi: (bi, qi, hi)),
        compiler_params=_params("parallel", "parallel", "arbitrary"),
        name="attn_ctx" if has_ctx else "attn",
    )(*args)


def _merge_kernel(attn_ref, cb_ref, cc_ref, cx_ref, ccp_ref, cxp_ref, ccn_ref, cxn_ref,
                  cw_ref, wa_ref, wc_ref, ga_ref, gc_ref, o_ref, conv_scr, *, seq_len):
    i = pl.program_id(0)
    j = pl.program_id(1)
    tm = cc_ref.shape[0]

    @pl.when(j == 0)
    def _():
        u = cc_ref[...].astype(F32) * cx_ref[...].astype(F32)
        u_before = (ccp_ref[...].astype(F32) * cxp_ref[...].astype(F32))[HALO_ROWS - 1:HALO_ROWS, :]
        u_after = (ccn_ref[...].astype(F32) * cxn_ref[...].astype(F32))[0:1, :]
        row = lax.broadcasted_iota(jnp.int32, (tm, 1), 0)
        pos = (i * tm + row) % seq_len
        u_prev = jnp.where(row == 0, u_before, pltpu.roll(u, 1, axis=0))
        u_prev = jnp.where(pos == 0, 0.0, u_prev)
        u_next = jnp.where(row == tm - 1, u_after, pltpu.roll(u, tm - 1, axis=0))
        u_next = jnp.where(pos == seq_len - 1, 0.0, u_next)
        cw = cw_ref[...]
        conv = u_prev * cw[0:1, :] + u * cw[1:2, :] + u_next * cw[2:3, :]
        conv_scr[...] = (cb_ref[...].astype(F32) * conv).astype(BF16)

    ya = jnp.dot(attn_ref[...], wa_ref[...], preferred_element_type=F32)
    yc = jnp.dot(conv_scr[...], wc_ref[...], preferred_element_type=F32)
    o_ref[...] = (ga_ref[...].astype(F32) * ya + gc_ref[...].astype(F32) * yc).astype(o_ref.dtype)


def _merge(attn, proj, conv_w, w_attn_out, w_conv_out, *, seq_len, cols, tm, tn=1024):
    t, aw = attn.shape
    d = w_attn_out.shape[1]
    cb0, cc0, cx0, ga0, gc0 = (cols[k] for k in ("cb", "cc", "cx", "ga", "gc"))
    hb = tm // HALO_ROWS
    last_halo = t // HALO_ROWS - 1
    cblk = lambda c0: pl.BlockSpec((tm, D_CONV), lambda i, j: (i, c0 // D_CONV))
    halo_prev = lambda c0: pl.BlockSpec(
        (HALO_ROWS, D_CONV), lambda i, j: (jnp.maximum(i * hb - 1, 0), c0 // D_CONV))
    halo_next = lambda c0: pl.BlockSpec(
        (HALO_ROWS, D_CONV), lambda i, j: (jnp.minimum((i + 1) * hb, last_halo), c0 // D_CONV))
    in_specs = [pl.BlockSpec((tm, aw), lambda i, j: (i, 0)),
                cblk(cb0), cblk(cc0), cblk(cx0),
                halo_prev(cc0), halo_prev(cx0), halo_next(cc0), halo_next(cx0),
                pl.BlockSpec(conv_w.shape, lambda i, j: (0, 0)),
                pl.BlockSpec((aw, tn), lambda i, j: (0, j)),
                pl.BlockSpec((D_CONV, tn), lambda i, j: (0, j)),
                pl.BlockSpec((tm, tn), lambda i, j: (i, ga0 // tn + j)),
                pl.BlockSpec((tm, tn), lambda i, j: (i, gc0 // tn + j))]
    return pl.pallas_call(
        functools.partial(_merge_kernel, seq_len=seq_len),
        out_shape=jax.ShapeDtypeStruct((t, d), BF16),
        grid=(t // tm, d // tn),
        in_specs=in_specs,
        out_specs=pl.BlockSpec((tm, tn), lambda i, j: (i, j)),
        scratch_shapes=[pltpu.VMEM((tm, D_CONV), BF16)],
        compiler_params=_params("parallel", "arbitrary"),
        name="merge",
    )(attn, proj, proj, proj, proj, proj, proj, proj, conv_w, w_attn_out, w_conv_out, proj, proj)


def _post_kernel(m_ref, x_ref, mod_ref, gpost_ref, gpre_ref, wo_ref, wr_ref,
                 x1_ref, h2_ref, lg_ref):
    y = jnp.dot(m_ref[...], wo_ref[...], preferred_element_type=F32)
    gate1 = mod_ref[0, 2:3, :]
    shift2 = mod_ref[0, 3:4, :]
    scale2 = mod_ref[0, 4:5, :]
    x1 = x_ref[...] + gate1 * (_rms(y) * gpost_ref[...])
    x1_ref[...] = x1
    h2 = (_rms(x1) * gpre_ref[...]) * (1 + scale2) + shift2
    h2_ref[...] = _pack_pairs(h2)
    lg_ref[...] = jnp.dot(h2.astype(BF16), wr_ref[...], preferred_element_type=F32)


def _post(merged, x, mods3, g_post, g_pre, w_o, w_router, *, mod_row, tm):
    t, d = x.shape
    e = w_router.shape[1]
    row = lambda i: (i, 0)
    const = lambda i: (0, 0)
    return pl.pallas_call(
        _post_kernel,
        out_shape=(jax.ShapeDtypeStruct((t, d), F32),
                   jax.ShapeDtypeStruct((t, d // 2), I32),
                   jax.ShapeDtypeStruct((t, e), F32)),
        grid=(t // tm,),
        in_specs=[pl.BlockSpec((tm, d), row), pl.BlockSpec((tm, d), row),
                  pl.BlockSpec((1,) + mods3.shape[1:], lambda i: (mod_row(i), 0, 0)),
                  pl.BlockSpec((1, d), const), pl.BlockSpec((1, d), const),
                  pl.BlockSpec((d, d), const), pl.BlockSpec((d, e), const)],
        out_specs=(pl.BlockSpec((tm, d), row), pl.BlockSpec((tm, d // 2), row),
                   pl.BlockSpec((tm, e), row)),
        compiler_params=_params("parallel"),
        name="post",
    )(merged, x, mods3, g_post, g_pre, w_o, w_router)


def _route_kernel(lg_ref, bias_ref, idx_ref, w_ref, rank_ref, cnt_ref, carry_scr):
    @pl.when(pl.program_id(0) == 0)
    def _():
        carry_scr[...] = jnp.zeros_like(carry_scr)

    s = jax.nn.sigmoid(lg_ref[...])
    biased = s + bias_ref[...]
    tm, e = s.shape
    per = e // N_GROUPS
    lane = lax.broadcasted_iota(I32, (tm, e), 1)
    grp = lane // per
    neg = -jnp.inf

    def first_argmax(v):
        m = jnp.max(v, axis=-1, keepdims=True)
        idx = jnp.min(jnp.where(v == m, lane, e), axis=-1, keepdims=True)
        return m, idx

    scores = []
    for g in range(N_GROUPS):
        vg = jnp.where(grp == g, biased, neg)
        m1, i1 = first_argmax(vg)
        m2 = jnp.max(jnp.where(lane == i1, neg, vg), axis=-1, keepdims=True)
        scores.append(m1 + m2)
    allowed = jnp.zeros((tm, e), jnp.bool_)
    for g in range(N_GROUPS):
        beaten_by = jnp.zeros((tm, 1), I32)
        for g2 in range(N_GROUPS):
            if g2 == g:
                continue
            wins = (scores[g2] > scores[g]) | ((scores[g2] == scores[g]) & (g2 < g))
            beaten_by = beaten_by + wins.astype(I32)
        allowed = allowed | ((grp == g) & (beaten_by < TOPK_GROUPS))
    masked = jnp.where(allowed, biased, neg)
    hits, ids = [], []
    for _ in range(TOP_K):
        _, idx = first_argmax(masked)
        hit = lane == idx
        hits.append(hit)
        ids.append(idx)
        masked = jnp.where(hit, neg, masked)
    chosen = functools.reduce(jnp.logical_or, hits)
    w = jnp.where(chosen, s, 0.0)
    w = w / jnp.sum(w, axis=-1, keepdims=True) * ROUTED_SCALE

    onehot = jnp.where(chosen, 1.0, 0.0)
    r_i = lax.broadcasted_iota(I32, (tm, tm), 0)
    c_i = lax.broadcasted_iota(I32, (tm, tm), 1)
    below = jnp.where(c_i < r_i, 1.0, 0.0).astype(BF16)
    rank = jnp.dot(below, onehot.astype(BF16), preferred_element_type=F32) + carry_scr[...]
    carry = carry_scr[...] + jnp.sum(onehot, axis=0, keepdims=True)
    carry_scr[...] = carry
    cnt_ref[...] = carry.astype(I32)

    col = lax.broadcasted_iota(I32, (tm, TOP_K), 1)
    idx_o = jnp.zeros((tm, TOP_K), I32)
    w_o = jnp.zeros((tm, TOP_K), F32)
    rank_o = jnp.zeros((tm, TOP_K), F32)
    for k in range(TOP_K):
        wk = jnp.sum(jnp.where(hits[k], w, 0.0), axis=-1, keepdims=True)
        rk = jnp.sum(jnp.where(hits[k], rank, 0.0), axis=-1, keepdims=True)
        idx_o = jnp.where(col == k, ids[k], idx_o)
        w_o = jnp.where(col == k, wk, w_o)
        rank_o = jnp.where(col == k, rk, rank_o)
    idx_ref[...] = idx_o
    w_ref[...] = w_o
    rank_ref[...] = rank_o.astype(I32)


def _route(logits, bias, *, tm):
    t, e = logits.shape
    row = lambda i: (i, 0)
    return pl.pallas_call(
        _route_kernel,
        out_shape=(jax.ShapeDtypeStruct((t, TOP_K), I32), jax.ShapeDtypeStruct((t, TOP_K), F32),
                   jax.ShapeDtypeStruct((t, TOP_K), I32), jax.ShapeDtypeStruct((1, e), I32)),
        grid=(t // tm,),
        in_specs=[pl.BlockSpec((tm, e), row), pl.BlockSpec((1, e), lambda i: (0, 0))],
        out_specs=(pl.BlockSpec((tm, TOP_K), row), pl.BlockSpec((tm, TOP_K), row),
                   pl.BlockSpec((tm, TOP_K), row), pl.BlockSpec((1, e), lambda i: (0, 0))),
        scratch_shapes=[pltpu.VMEM((1, e), F32)],
        compiler_params=_params("arbitrary"),
        name="route",
    )(logits, bias)


def _slots_kernel(idx_ref, rank_ref, start_ref, o_ref):
    idx = idx_ref[...]
    tm = idx.shape[0]
    e = start_ref.shape[1]
    lane = lax.broadcasted_iota(I32, (tm, e), 1)
    col = lax.broadcasted_iota(I32, (tm, TOP_K), 1)
    base = jnp.zeros((tm, TOP_K), I32)
    for k in range(TOP_K):
        sk = jnp.sum(jnp.where(lane == idx[:, k:k + 1], start_ref[...], 0), axis=-1, keepdims=True)
        base = jnp.where(col == k, sk, base)
    o_ref[...] = base + rank_ref[...]


def _slots(idx, rank, start, *, tm):
    t = idx.shape[0]
    row = lambda i: (i, 0)
    return pl.pallas_call(
        _slots_kernel,
        out_shape=jax.ShapeDtypeStruct((t, TOP_K), I32),
        grid=(t // tm,),
        in_specs=[pl.BlockSpec((tm, TOP_K), row), pl.BlockSpec((tm, TOP_K), row),
                  pl.BlockSpec(start.shape, lambda i: (0, 0))],
        out_specs=pl.BlockSpec((tm, TOP_K), row),
        compiler_params=_params("parallel"),
        name="slots",
    )(idx, rank, start)


def _row_copy(src_ref, src_row, dst_ref, dst_row, sem):
    return pltpu.make_async_copy(src_ref.at[pl.ds(src_row, 1)], dst_ref.at[pl.ds(dst_row, 1)], sem)


def _dispatch_kernel(pos_ref, x_ref, zero_ref, xs_ref, sem):
    del zero_ref
    tm = x_ref.shape[0]

    def body(r, carry):
        for k in range(TOP_K):
            _row_copy(x_ref, r, xs_ref, pos_ref[r * TOP_K + k], sem).start()
        return carry

    lax.fori_loop(0, tm, body, 0)
    n = tm * TOP_K
    pltpu.make_async_copy(xs_ref.at[pl.ds(0, n)], xs_ref.at[pl.ds(0, n)], sem).wait()


def _dispatch(pos_flat, h2u, n_rows, *, tm):
    t, dw = h2u.shape
    zeros = jnp.zeros((n_rows, dw), I32)
    return pl.pallas_call(
        _dispatch_kernel,
        out_shape=jax.ShapeDtypeStruct((n_rows, dw), I32),
        grid=(t // tm,),
        in_specs=[pl.BlockSpec((tm * TOP_K,), lambda i: (i,), memory_space=pltpu.SMEM),
                  pl.BlockSpec((tm, dw), lambda i: (i, 0)),
                  pl.BlockSpec(memory_space=pl.ANY)],
        out_specs=pl.BlockSpec(memory_space=pl.ANY),
        scratch_shapes=[pltpu.SemaphoreType.DMA],
        input_output_aliases={2: 0},
        compiler_params=_params("arbitrary"),
        name="dispatch",
    )(pos_flat, h2u, zeros)


def _ffn(x, wg, wu, wd):
    a = jnp.dot(x, wg, preferred_element_type=F32)
    b = jnp.dot(x, wu, preferred_element_type=F32)
    hid = (a * jax.nn.sigmoid(a)) * b
    return jnp.dot(hid.astype(BF16), wd, preferred_element_type=F32)


def _expert_kernel(te_ref, nv_ref, x_ref, wg_ref, wu_ref, wd_ref, o_ref):
    del te_ref

    live = pl.program_id(0) < nv_ref[0]

    @pl.when(live)
    def _():
        x = _unpack_pairs(x_ref[...]).astype(BF16)
        o_ref[...] = _pack_pairs(_ffn(x, wg_ref[0], wu_ref[0], wd_ref[0]))

    @pl.when(jnp.logical_not(live))
    def _():
        o_ref[...] = jnp.zeros_like(o_ref)


def _experts(tile_expert, n_valid, xs, wg, wu, wd):
    n_rows, dw = xs.shape
    ne, d, de = wg.shape
    tm = EXPERT_TILE
    row = lambda j, te, nv: (jnp.minimum(j, nv[0] - 1), 0)
    wsel = lambda j, te, nv: (te[j], 0, 0)
    return pl.pallas_call(
        _expert_kernel,
        out_shape=jax.ShapeDtypeStruct((n_rows, dw), I32),
        grid_spec=pltpu.PrefetchScalarGridSpec(
            num_scalar_prefetch=2,
            grid=(n_rows // tm,),
            in_specs=[pl.BlockSpec((tm, dw), row),
                      pl.BlockSpec((1, d, de), wsel), pl.BlockSpec((1, d, de), wsel),
                      pl.BlockSpec((1, de, d), wsel)],
            out_specs=pl.BlockSpec((tm, dw), lambda j, te, nv: (j, 0))),
        compiler_params=_params("arbitrary"),
        name="experts",
    )(tile_expert, n_valid, xs, wg, wu, wd)


def _combine_kernel(pos_ref, w_ref, h_ref, x1_ref, mod_ref, gpost_ref, sg_ref, su_ref, sd_ref,
                    ys_ref, o_ref, buf, sem):
    tm = h_ref.shape[0]

    def body(r, carry):
        for k in range(TOP_K):
            _row_copy(ys_ref, pos_ref[r * TOP_K + k], buf, k * tm + r, sem).start()
        return carry

    lax.fori_loop(0, tm, body, 0)
    h = _unpack_pairs(h_ref[...]).astype(BF16)
    acc = _ffn(h, sg_ref[...], su_ref[...], sd_ref[...])
    n = tm * TOP_K
    pltpu.make_async_copy(ys_ref.at[pl.ds(0, n)], buf, sem).wait()
    w = w_ref[...]
    for k in range(TOP_K):
        acc = acc + w[:, k:k + 1] * _unpack_pairs(buf[k * tm:(k + 1) * tm, :])
    gate2 = mod_ref[0, 5:6, :]
    o_ref[...] = x1_ref[...] + gate2 * (_rms(acc) * gpost_ref[...])


def _combine(pos_flat, w, h2u, x1, mods3, g_post, sg, su, sd, ys, *, mod_row, tm):
    t, d = x1.shape
    dw = h2u.shape[1]
    ds = sg.shape[1]
    row = lambda i: (i, 0)
    const = lambda i: (0, 0)
    return pl.pallas_call(
        _combine_kernel,
        out_shape=jax.ShapeDtypeStruct((t, d), F32),
        grid=(t // tm,),
        in_specs=[pl.BlockSpec((tm * TOP_K,), lambda i: (i,), memory_space=pltpu.SMEM),
                  pl.BlockSpec((tm, TOP_K), row), pl.BlockSpec((tm, dw), row),
                  pl.BlockSpec((tm, d), row),
                  pl.BlockSpec((1,) + mods3.shape[1:], lambda i: (mod_row(i), 0, 0)),
                  pl.BlockSpec((1, d), const),
                  pl.BlockSpec((d, ds), const), pl.BlockSpec((d, ds), const),
                  pl.BlockSpec((ds, d), const),
                  pl.BlockSpec(memory_space=pl.ANY)],
        out_specs=pl.BlockSpec((tm, d), row),
        scratch_shapes=[pltpu.VMEM((tm * TOP_K, dw), I32), pltpu.SemaphoreType.DMA],
        compiler_params=_params("arbitrary"),
        name="combine",
    )(pos_flat, w, h2u, x1, mods3, g_post, sg, su, sd, ys)


def _rope_tables(n_tokens):
    rows = n_tokens // GRID_W
    row = jnp.repeat(jnp.arange(rows, dtype=F32), GRID_W)
    col = jnp.tile(jnp.arange(GRID_W, dtype=F32), rows)
    n_freq = HEAD_DIM // 4
    freqs = ROPE_THETA ** (-jnp.arange(n_freq, dtype=F32) / n_freq)
    ang_r = row[:, None] * freqs
    ang_c = col[:, None] * freqs
    ang = jnp.concatenate([ang_r, ang_r, ang_c, ang_c], axis=-1)
    sign = jnp.where((jnp.arange(HEAD_DIM) % (HEAD_DIM // 2)) < HEAD_DIM // 4, -1.0, 1.0)
    return jnp.cos(ang), jnp.sin(ang) * sign


def _mixer(x3, mods3, mod_row, wts, rope_tabs, ctx, *, proj_dtype, tm, tq):
    b, n, d = x3.shape
    x = x3.reshape(b * n, d)
    qk_cols = 4 * N_HEADS * HEAD_DIM
    cols = {"cb": qk_cols + N_HEADS * V_HEAD_DIM}
    cols["cc"] = cols["cb"] + D_CONV
    cols["cx"] = cols["cc"] + D_CONV
    cols["ga"] = cols["cx"] + D_CONV
    cols["gc"] = cols["ga"] + d
    proj = _proj(x, mods3, wts["g_pre_mix"], wts["w_in"], mod_row=mod_row, rope_tabs=rope_tabs,
                 out_dtype=proj_dtype, gate_col=cols["ga"], rope_cols=qk_cols, tm=tm)
    attn = _attention(proj.reshape(b, n, -1), wts["lam_vecs"], wts["g_subln"], ctx, tq=tq)
    merged = _merge(attn.reshape(b * n, -1), proj, wts["conv_w"], wts["w_attn_out"],
                    wts["w_conv_out"], seq_len=n, cols=cols, tm=tm)
    x1, h2u, logits = _post(merged, x, mods3, wts["g_post_mix"], wts["g_pre_ffn"], wts["w_o"],
                            wts["w_router"], mod_row=mod_row, tm=tm)
    return x1, h2u, logits, proj


def _moe(x1, h2u, logits, mods3, mod_row, wts, *, tm):
    t = x1.shape[0]
    idx, w, rank, counts = _route(logits, wts["router_bias"], tm=512)
    tiles_per = (counts[0] + EXPERT_TILE - 1) // EXPERT_TILE
    tile_end = jnp.cumsum(tiles_per)
    start = ((tile_end - tiles_per) * EXPERT_TILE)[None].astype(I32)
    n_tiles = t * TOP_K // EXPERT_TILE + N_EXPERTS
    tile_expert = jnp.minimum(
        jnp.sum(jnp.arange(n_tiles, dtype=I32)[:, None] >= tile_end[None, :], axis=1),
        N_EXPERTS - 1).astype(I32)
    n_valid = tile_end[-1:].astype(I32)
    pos = _slots(idx, rank, start, tm=512).reshape(t * TOP_K)
    xs = _dispatch(pos, h2u, n_tiles * EXPERT_TILE, tm=tm)
    ys = _experts(tile_expert, n_valid, xs, wts["w_exp_gate"], wts["w_exp_up"], wts["w_exp_down"])
    return _combine(pos, w, h2u, x1, mods3, wts["g_post_ffn"], wts["w_sh_gate"], wts["w_sh_up"],
                    wts["w_sh_down"], ys, mod_row=mod_row, tm=tm)


def kernel(x_prompt, x_sample, cache_k, cache_v, c, c_ctx, w_ada, b_ada, g_pre_mix, g_post_mix, w_in, lambda_q1, lambda_k1, lambda_q2, lambda_k2, g_subln, conv_w, w_attn_out, w_conv_out, w_o, g_pre_ffn, g_post_ffn, w_router, router_bias, w_exp_gate, w_exp_up, w_exp_down, w_sh_gate, w_sh_up, w_sh_down):
    bp, sp, d = x_prompt.shape
    bs, ss, _ = x_sample.shape
    past = cache_k.shape[2]
    l = 0
    wts = {
        "g_pre_mix": g_pre_mix[l][None], "g_post_mix": g_post_mix[l][None],
        "g_pre_ffn": g_pre_ffn[l][None], "g_post_ffn": g_post_ffn[l][None],
        "g_subln": g_subln[l][None], "conv_w": conv_w[l],
        "lam_vecs": jnp.stack([lambda_q1[l], lambda_k1[l], lambda_q2[l], lambda_k2[l]]),
        "router_bias": router_bias[l][None],
        "w_in": w_in[l].astype(BF16), "w_attn_out": w_attn_out[l].astype(BF16),
        "w_conv_out": w_conv_out[l].astype(BF16), "w_o": w_o[l].astype(BF16),
        "w_router": w_router[l].astype(BF16),
        "w_exp_gate": w_exp_gate[l].astype(BF16), "w_exp_up": w_exp_up[l].astype(BF16),
        "w_exp_down": w_exp_down[l].astype(BF16),
        "w_sh_gate": w_sh_gate[l].astype(BF16), "w_sh_up": w_sh_up[l].astype(BF16),
        "w_sh_down": w_sh_down[l].astype(BF16),
    }
    pad_rows = 8 - (bs + 1)
    c_rows = jnp.concatenate([c, c_ctx[None], jnp.zeros((pad_rows, d), F32)], axis=0)
    mods3 = _ada(c_rows, w_ada[l], b_ada[l][None]).reshape(8, 6, d)

    tm = 512
    x1p, h2p, lgp, proj_p = _mixer(x_prompt, mods3, lambda i: bs, wts, None, None,
                                   proj_dtype=F32, tm=tm, tq=sp)
    hk = 2 * N_HEADS * HEAD_DIM
    new_k = proj_p[:, hk:2 * hk].reshape(bp, 1, sp, 2, N_HEADS, HEAD_DIM)
    new_v = proj_p[:, 2 * hk:2 * hk + N_HEADS * V_HEAD_DIM].reshape(bp, 1, sp, N_HEADS, V_HEAD_DIM)

    ctx = (cache_k[:, l].reshape(bs, past, hk).astype(BF16),
           cache_v[:, l].reshape(bs, past, N_HEADS * V_HEAD_DIM).astype(BF16))
    x1s, h2s, lgs, _ = _mixer(x_sample, mods3, lambda i: (i * tm) // ss, wts, _rope_tables(ss), ctx,
                              proj_dtype=BF16, tm=tm, tq=256)

    tm_moe = 256
    n_lat = bs * ss
    y = _moe(jnp.concatenate([x1s, x1p]), jnp.concatenate([h2s, h2p]), jnp.concatenate([lgs, lgp]),
             mods3, lambda i: jnp.minimum((i * tm_moe) // ss, bs), wts, tm=tm_moe)
    return (y[n_lat:].reshape(bp, sp, d), y[:n_lat].reshape(bs, ss, d), new_k, new_v)
```

```python
import functools
import math

import jax
import jax.numpy as jnp
from jax import lax
from jax.experimental import pallas as pl
from jax.experimental.pallas import tpu as pltpu

F32 = jnp.float32
BF16 = jnp.bfloat16
I32 = jnp.int32

EPS = 1e-6
N_HEADS = 8
HEAD_DIM = 128
V_HEAD_DIM = 2 * HEAD_DIM
D_CONV = 1024
N_EXPERTS = 64
TOP_K = 8
N_GROUPS = 8
TOPK_GROUPS = 4
ROUTED_SCALE = 2.5
ROPE_THETA = 10000.0
GRID_W = 64
LAM_INIT = 0.8 - 0.6 * math.exp(-0.3 * 0)

LANES = 128
HALO_ROWS = 16
VMEM_LIMIT_BYTES = 56 * 1024 * 1024
EXPERT_TILE = 256
KEY_CHUNK = 512
QUERY_TILE = 256
PROJ_TILE = (1024, 1024)
ROW_TILE = 512
POST_TILE = 256
DISPATCH_TILE = 256
COMBINE_TILE = 128
ROUTE_TILE = 512


def _params(*sem):
    return pltpu.CompilerParams(dimension_semantics=sem, vmem_limit_bytes=VMEM_LIMIT_BYTES)


def _rms(x):
    return x * lax.rsqrt(jnp.mean(x * x, axis=-1, keepdims=True) + EPS)


def _ada_kernel(c_ref, w_ref, b_ref, o_ref):
    c = c_ref[...]
    a = (c * jax.nn.sigmoid(c)).astype(BF16)
    o_ref[...] = jnp.dot(a, w_ref[...].astype(BF16), preferred_element_type=F32) + b_ref[...]


def _ada(c_rows, w_ada, b_ada, tn=1024):
    r, d = c_rows.shape
    n = w_ada.shape[1]
    return pl.pallas_call(
        _ada_kernel,
        out_shape=jax.ShapeDtypeStruct((r, n), F32),
        grid=(n // tn,),
        in_specs=[pl.BlockSpec((r, d), lambda j: (0, 0)),
                  pl.BlockSpec((d, tn), lambda j: (0, j)),
                  pl.BlockSpec((1, tn), lambda j: (0, j))],
        out_specs=pl.BlockSpec((r, tn), lambda j: (0, j)),
        compiler_params=_params("arbitrary"),
        name="ada",
    )(c_rows, w_ada, b_ada)


def _proj_kernel(*refs, rope, n_rope_tiles, sig_start):
    if rope:
        x_ref, mod_ref, g_ref, w_ref, cos_ref, sin_ref, o_ref, h_scr = refs
    else:
        x_ref, mod_ref, g_ref, w_ref, o_ref, h_scr = refs
    j = pl.program_id(1)

    @pl.when(j == 0)
    def _():
        shift = mod_ref[0, 0:1, :]
        scale = mod_ref[0, 1:2, :]
        h = (_rms(x_ref[...]) * g_ref[...]) * (1 + scale) + shift
        h_scr[...] = h.astype(BF16)

    acc = jnp.dot(h_scr[...], w_ref[...], preferred_element_type=F32)
    tn = acc.shape[1]

    if rope:
        @pl.when(j < n_rope_tiles)
        def _():
            cos = cos_ref[...]
            sin = sin_ref[...]
            lane = lax.broadcasted_iota(jnp.int32, cos.shape, 1)
            first = (lane % (HEAD_DIM // 2)) < (HEAD_DIM // 4)
            for s in range(tn // HEAD_DIM):
                xs = acc[:, s * HEAD_DIM:(s + 1) * HEAD_DIM]
                partner = jnp.where(first,
                                    pltpu.roll(xs, HEAD_DIM - HEAD_DIM // 4, axis=1),
                                    pltpu.roll(xs, HEAD_DIM // 4, axis=1))
                o_ref[:, s * HEAD_DIM:(s + 1) * HEAD_DIM] = (xs * cos + partner * sin).astype(o_ref.dtype)
        plain_lo = n_rope_tiles
    else:
        plain_lo = 0

    @pl.when((j >= plain_lo) & (j < sig_start))
    def _():
        o_ref[...] = acc.astype(o_ref.dtype)

    @pl.when(j >= sig_start)
    def _():
        o_ref[...] = jax.nn.sigmoid(acc).astype(o_ref.dtype)


def _proj(x, mods3, g, w, *, mod_row, rope_tabs, out_dtype, gate_col, rope_cols, tm, tn=512):
    t, d = x.shape
    n = w.shape[1]
    rope = rope_tabs is not None
    in_specs = [pl.BlockSpec((tm, d), lambda i, j: (i, 0)),
                pl.BlockSpec((1,) + mods3.shape[1:], lambda i, j: (mod_row(i), 0, 0)),
                pl.BlockSpec((1, d), lambda i, j: (0, 0)),
                pl.BlockSpec((d, tn), lambda i, j: (0, j))]
    args = [x, mods3, g, w]
    if rope:
        cos, sin = rope_tabs
        nblk = cos.shape[0] // tm
        in_specs += [pl.BlockSpec((tm, HEAD_DIM), lambda i, j: (i % nblk, 0)),
                     pl.BlockSpec((tm, HEAD_DIM), lambda i, j: (i % nblk, 0))]
        args += [cos, sin]
    kern = functools.partial(_proj_kernel, rope=rope, n_rope_tiles=rope_cols // tn,
                             sig_start=gate_col // tn)
    return pl.pallas_call(
        kern,
        out_shape=jax.ShapeDtypeStruct((t, n), out_dtype),
        grid=(t // tm, n // tn),
        in_specs=in_specs,
        out_specs=pl.BlockSpec((tm, tn), lambda i, j: (i, j)),
        scratch_shapes=[pltpu.VMEM((tm, d), BF16)],
        compiler_params=_params("parallel", "arbitrary"),
        name="proj_rope" if rope else "proj",
    )(*args)


def _attn_kernel(*refs, has_ctx):
    if has_ctx:
        (lam_ref, gs_ref, q1_ref, q2_ref, k1_ref, k2_ref, v_ref,
         k1c_ref, k2c_ref, vc_ref, o_ref, s_even, s_odd, m_even, m_odd) = refs
    else:
        (lam_ref, gs_ref, q1_ref, q2_ref, k1_ref, k2_ref, v_ref,
         o_ref, s_even, s_odd, m_even, m_odd) = refs
    t = pl.program_id(1)
    lv = lam_ref[...]
    lam = (jnp.exp(jnp.sum(lv[0:1] * lv[1:2], axis=-1, keepdims=True))
           - jnp.exp(jnp.sum(lv[2:3] * lv[3:4], axis=-1, keepdims=True)) + LAM_INIT)
    c = (HEAD_DIM ** -0.5) * math.log2(math.e)
    nt = (((1,), (1,)), ((), ()))
    tq = q1_ref.shape[1]

    chunks = []
    off = 0
    groups = [(k1_ref, k2_ref, v_ref)] + ([(k1c_ref, k2c_ref, vc_ref)] if has_ctx else [])
    for ka, kb, vv in groups:
        n = ka.shape[1]
        ch = min(KEY_CHUNK, n)
        for c0 in range(0, n, ch):
            chunks.append((ka, kb, vv, c0, ch, off + c0))
        off += n

    @pl.when(t == 0)
    def _():
        s_odd[...] = jnp.zeros_like(s_odd)
        m_odd[...] = jnp.zeros_like(m_odd)

    def step(s_fill, m_fill, s_drain, m_drain):
        qs = (q1_ref[0].astype(BF16), q2_ref[0].astype(BF16))
        mbs = (m_drain[0], m_drain[1])
        mrun = [jnp.full((tq, LANES), -jnp.inf, F32)] * 2
        lrun = [jnp.zeros((tq, LANES), F32)] * 2
        acc = [jnp.zeros((tq, V_HEAD_DIM), F32)] * 2
        for ka, kb, vv, c0, ch, col in chunks:
            for mp in range(2):
                k = (ka, kb)[mp][0, c0:c0 + ch, :].astype(BF16)
                s = lax.dot_general(qs[mp], k, nt, preferred_element_type=F32)
                s_fill[mp, :, col:col + ch] = s
                for g in range(ch // LANES):
                    mrun[mp] = jnp.maximum(mrun[mp], s[:, g * LANES:(g + 1) * LANES])
            v = vv[0, c0:c0 + ch, :].astype(BF16)
            for mp in range(2):
                ps = []
                for g in range(ch // LANES):
                    sg = s_drain[mp, :, col + g * LANES:col + (g + 1) * LANES]
                    pg = jnp.exp2((sg - mbs[mp]) * c)
                    lrun[mp] = lrun[mp] + pg
                    ps.append(pg.astype(BF16))
                acc[mp] = acc[mp] + jnp.dot(jnp.concatenate(ps, axis=1), v, preferred_element_type=F32)
        for mp in range(2):
            m_fill[mp] = jnp.broadcast_to(jnp.max(mrun[mp], axis=-1, keepdims=True), (tq, LANES))
        l1 = jnp.sum(lrun[0], axis=-1, keepdims=True)
        l2 = jnp.sum(lrun[1], axis=-1, keepdims=True)
        o = acc[0] * (1.0 / l1) - acc[1] * (lam / l2)
        o_ref[0] = ((_rms(o) * gs_ref[...]) * (1 - LAM_INIT)).astype(o_ref.dtype)

    @pl.when(t % 2 == 0)
    def _():
        step(s_even, m_even, s_odd, m_odd)

    @pl.when(t % 2 == 1)
    def _():
        step(s_odd, m_odd, s_even, m_even)


def _attention(proj3, lam_vecs, g_subln, ctx, *, tq):
    b, n, _ = proj3.shape
    h = N_HEADS
    nq = n // tq
    n_tiles = h * nq
    has_ctx = ctx is not None
    f_head = lambda t: jnp.minimum(t, n_tiles - 1) // nq
    f_row = lambda t: jnp.minimum(t, n_tiles - 1) % nq
    d_head = lambda t: jnp.maximum(t - 1, 0) // nq
    d_row = lambda t: jnp.maximum(t - 1, 0) % nq
    q_spec = lambda off: pl.BlockSpec((1, tq, HEAD_DIM), lambda bi, t: (bi, f_row(t), off + f_head(t)))
    k_spec = lambda rows, off: pl.BlockSpec((1, rows, HEAD_DIM), lambda bi, t: (bi, 0, off + f_head(t)))
    v_spec = lambda rows, off: pl.BlockSpec((1, rows, V_HEAD_DIM), lambda bi, t: (bi, 0, off + d_head(t)))
    in_specs = [pl.BlockSpec((4, HEAD_DIM), lambda bi, t: (0, 0)),
                pl.BlockSpec((1, V_HEAD_DIM), lambda bi, t: (0, 0)),
                q_spec(0), q_spec(h), k_spec(n, 2 * h), k_spec(n, 3 * h), v_spec(n, 2 * h)]
    args = [lam_vecs, g_subln, proj3, proj3, proj3, proj3, proj3]
    n_keys = n
    if has_ctx:
        ck, cv = ctx
        p = ck.shape[1]
        n_keys += p
        in_specs += [k_spec(p, 0), k_spec(p, h), v_spec(p, 0)]
        args += [ck, ck, cv]
    scores = pltpu.VMEM((2, tq, n_keys), F32)
    maxima = pltpu.VMEM((2, tq, LANES), F32)
    return pl.pallas_call(
        functools.partial(_attn_kernel, has_ctx=has_ctx),
        out_shape=jax.ShapeDtypeStruct((b, n, h * V_HEAD_DIM), BF16),
        grid=(b, n_tiles + 1),
        in_specs=in_specs,
        out_specs=pl.BlockSpec((1, tq, V_HEAD_DIM), lambda bi, t: (bi, d_row(t), d_head(t))),
        scratch_shapes=[scores, scores, maxima, maxima],
        compiler_params=_params("parallel", "arbitrary"),
        name="attn_ctx" if has_ctx else "attn",
    )(*args)


def _merge_kernel(attn_ref, cb_ref, cc_ref, cx_ref, ccp_ref, cxp_ref, ccn_ref, cxn_ref,
                  cw_ref, wa_ref, wc_ref, ga_ref, gc_ref, o_ref, conv_scr, *, seq_len):
    i = pl.program_id(0)
    j = pl.program_id(1)
    tm = cc_ref.shape[0]

    @pl.when(j == 0)
    def _():
        u = cc_ref[...].astype(F32) * cx_ref[...].astype(F32)
        u_before = (ccp_ref[...].astype(F32) * cxp_ref[...].astype(F32))[HALO_ROWS - 1:HALO_ROWS, :]
        u_after = (ccn_ref[...].astype(F32) * cxn_ref[...].astype(F32))[0:1, :]
        row = lax.broadcasted_iota(jnp.int32, (tm, 1), 0)
        pos = (i * tm + row) % seq_len
        u_prev = jnp.where(row == 0, u_before, pltpu.roll(u, 1, axis=0))
        u_prev = jnp.where(pos == 0, 0.0, u_prev)
        u_next = jnp.where(row == tm - 1, u_after, pltpu.roll(u, tm - 1, axis=0))
        u_next = jnp.where(pos == seq_len - 1, 0.0, u_next)
        cw = cw_ref[...]
        conv = u_prev * cw[0:1, :] + u * cw[1:2, :] + u_next * cw[2:3, :]
        conv_scr[...] = (cb_ref[...].astype(F32) * conv).astype(BF16)

    ya = jnp.dot(attn_ref[...], wa_ref[...], preferred_element_type=F32)
    yc = jnp.dot(conv_scr[...], wc_ref[...], preferred_element_type=F32)
    o_ref[...] = (ga_ref[...].astype(F32) * ya + gc_ref[...].astype(F32) * yc).astype(o_ref.dtype)


def _merge(attn, proj, conv_w, w_attn_out, w_conv_out, *, seq_len, cols, tm, tn=1024):
    t, aw = attn.shape
    d = w_attn_out.shape[1]
    cb0, cc0, cx0, ga0, gc0 = (cols[k] for k in ("cb", "cc", "cx", "ga", "gc"))
    hb = tm // HALO_ROWS
    last_halo = t // HALO_ROWS - 1
    cblk = lambda c0: pl.BlockSpec((tm, D_CONV), lambda i, j: (i, c0 // D_CONV))
    halo_prev = lambda c0: pl.BlockSpec(
        (HALO_ROWS, D_CONV), lambda i, j: (jnp.maximum(i * hb - 1, 0), c0 // D_CONV))
    halo_next = lambda c0: pl.BlockSpec(
        (HALO_ROWS, D_CONV), lambda i, j: (jnp.minimum((i + 1) * hb, last_halo), c0 // D_CONV))
    in_specs = [pl.BlockSpec((tm, aw), lambda i, j: (i, 0)),
                cblk(cb0), cblk(cc0), cblk(cx0),
                halo_prev(cc0), halo_prev(cx0), halo_next(cc0), halo_next(cx0),
                pl.BlockSpec(conv_w.shape, lambda i, j: (0, 0)),
                pl.BlockSpec((aw, tn), lambda i, j: (0, j)),
                pl.BlockSpec((D_CONV, tn), lambda i, j: (0, j)),
                pl.BlockSpec((tm, tn), lambda i, j: (i, ga0 // tn + j)),
                pl.BlockSpec((tm, tn), lambda i, j: (i, gc0 // tn + j))]
    return pl.pallas_call(
        functools.partial(_merge_kernel, seq_len=seq_len),
        out_shape=jax.ShapeDtypeStruct((t, d), BF16),
        grid=(t // tm, d // tn),
        in_specs=in_specs,
        out_specs=pl.BlockSpec((tm, tn), lambda i, j: (i, j)),
        scratch_shapes=[pltpu.VMEM((tm, D_CONV), BF16)],
        compiler_params=_params("parallel", "arbitrary"),
        name="merge",
    )(attn, proj, proj, proj, proj, proj, proj, proj, conv_w, w_attn_out, w_conv_out, proj, proj)


def _post_kernel(m_ref, x_ref, mod_ref, gpost_ref, gpre_ref, wo_ref, wr_ref,
                 x1_ref, h2_ref, lg_ref):
    y = jnp.dot(m_ref[...], wo_ref[...], preferred_element_type=F32)
    gate1 = mod_ref[0, 2:3, :]
    shift2 = mod_ref[0, 3:4, :]
    scale2 = mod_ref[0, 4:5, :]
    x1 = x_ref[...] + gate1 * (_rms(y) * gpost_ref[...])
    x1_ref[...] = x1
    h2 = (_rms(x1) * gpre_ref[...]) * (1 + scale2) + shift2
    h2_ref[...] = h2
    lg_ref[...] = jnp.dot(h2.astype(BF16), wr_ref[...], preferred_element_type=F32)


def _post(merged, x, mods3, g_post, g_pre, w_o, w_router, *, mod_row, tm):
    t, d = x.shape
    e = w_router.shape[1]
    row = lambda i: (i, 0)
    const = lambda i: (0, 0)
    return pl.pallas_call(
        _post_kernel,
        out_shape=(jax.ShapeDtypeStruct((t, d), F32),
                   jax.ShapeDtypeStruct((t, d), F32),
                   jax.ShapeDtypeStruct((t, e), F32)),
        grid=(t // tm,),
        in_specs=[pl.BlockSpec((tm, d), row), pl.BlockSpec((tm, d), row),
                  pl.BlockSpec((1,) + mods3.shape[1:], lambda i: (mod_row(i), 0, 0)),
                  pl.BlockSpec((1, d), const), pl.BlockSpec((1, d), const),
                  pl.BlockSpec((d, d), const), pl.BlockSpec((d, e), const)],
        out_specs=(pl.BlockSpec((tm, d), row), pl.BlockSpec((tm, d), row),
                   pl.BlockSpec((tm, e), row)),
        compiler_params=_params("parallel"),
        name="post",
    )(merged, x, mods3, g_post, g_pre, w_o, w_router)


def _route_kernel(lg_ref, bias_ref, idx_ref, w_ref, rank_ref, cnt_ref, carry_scr):
    @pl.when(pl.program_id(0) == 0)
    def _():
        carry_scr[...] = jnp.zeros_like(carry_scr)

    s = jax.nn.sigmoid(lg_ref[...])
    biased = s + bias_ref[...]
    tm, e = s.shape
    per = e // N_GROUPS
    lane = lax.broadcasted_iota(I32, (tm, e), 1)
    grp = lane // per
    neg = -jnp.inf

    def first_argmax(v):
        m = jnp.max(v, axis=-1, keepdims=True)
        idx = jnp.min(jnp.where(v == m, lane, e), axis=-1, keepdims=True)
        return m, idx

    scores = []
    for g in range(N_GROUPS):
        vg = jnp.where(grp == g, biased, neg)
        m1, i1 = first_argmax(vg)
        m2 = jnp.max(jnp.where(lane == i1, neg, vg), axis=-1, keepdims=True)
        scores.append(m1 + m2)
    allowed = jnp.zeros((tm, e), jnp.bool_)
    for g in range(N_GROUPS):
        beaten_by = jnp.zeros((tm, 1), I32)
        for g2 in range(N_GROUPS):
            if g2 == g:
                continue
            wins = (scores[g2] > scores[g]) | ((scores[g2] == scores[g]) & (g2 < g))
            beaten_by = beaten_by + wins.astype(I32)
        allowed = allowed | ((grp == g) & (beaten_by < TOPK_GROUPS))
    masked = jnp.where(allowed, biased, neg)
    hits, ids = [], []
    for _ in range(TOP_K):
        _, idx = first_argmax(masked)
        hit = lane == idx
        hits.append(hit)
        ids.append(idx)
        masked = jnp.where(hit, neg, masked)
    chosen = functools.reduce(jnp.logical_or, hits)
    w = jnp.where(chosen, s, 0.0)
    w = w / jnp.sum(w, axis=-1, keepdims=True) * ROUTED_SCALE

    onehot = jnp.where(chosen, 1.0, 0.0)
    r_i = lax.broadcasted_iota(I32, (tm, tm), 0)
    c_i = lax.broadcasted_iota(I32, (tm, tm), 1)
    below = jnp.where(c_i < r_i, 1.0, 0.0).astype(BF16)
    rank = jnp.dot(below, onehot.astype(BF16), preferred_element_type=F32) + carry_scr[...]
    carry = carry_scr[...] + jnp.sum(onehot, axis=0, keepdims=True)
    carry_scr[...] = carry
    cnt_ref[...] = carry.astype(I32)

    col = lax.broadcasted_iota(I32, (tm, TOP_K), 1)
    idx_o = jnp.zeros((tm, TOP_K), I32)
    w_o = jnp.zeros((tm, TOP_K), F32)
    rank_o = jnp.zeros((tm, TOP_K), F32)
    for k in range(TOP_K):
        wk = jnp.sum(jnp.where(hits[k], w, 0.0), axis=-1, keepdims=True)
        rk = jnp.sum(jnp.where(hits[k], rank, 0.0), axis=-1, keepdims=True)
        idx_o = jnp.where(col == k, ids[k], idx_o)
        w_o = jnp.where(col == k, wk, w_o)
        rank_o = jnp.where(col == k, rk, rank_o)
    idx_ref[...] = idx_o
    w_ref[...] = w_o
    rank_ref[...] = rank_o.astype(I32)


def _route(logits, bias, *, tm):
    t, e = logits.shape
    row = lambda i: (i, 0)
    return pl.pallas_call(
        _route_kernel,
        out_shape=(jax.ShapeDtypeStruct((t, TOP_K), I32), jax.ShapeDtypeStruct((t, TOP_K), F32),
                   jax.ShapeDtypeStruct((t, TOP_K), I32), jax.ShapeDtypeStruct((1, e), I32)),
        grid=(t // tm,),
        in_specs=[pl.BlockSpec((tm, e), row), pl.BlockSpec((1, e), lambda i: (0, 0))],
        out_specs=(pl.BlockSpec((tm, TOP_K), row), pl.BlockSpec((tm, TOP_K), row),
                   pl.BlockSpec((tm, TOP_K), row), pl.BlockSpec((1, e), lambda i: (0, 0))),
        scratch_shapes=[pltpu.VMEM((1, e), F32)],
        compiler_params=_params("arbitrary"),
        name="route",
    )(logits, bias)


def _slots_kernel(idx_ref, rank_ref, start_ref, o_ref):
    idx = idx_ref[...]
    tm = idx.shape[0]
    e = start_ref.shape[1]
    lane = lax.broadcasted_iota(I32, (tm, e), 1)
    col = lax.broadcasted_iota(I32, (tm, TOP_K), 1)
    base = jnp.zeros((tm, TOP_K), I32)
    for k in range(TOP_K):
        sk = jnp.sum(jnp.where(lane == idx[:, k:k + 1], start_ref[...], 0), axis=-1, keepdims=True)
        base = jnp.where(col == k, sk, base)
    o_ref[...] = base + rank_ref[...]


def _slots(idx, rank, start, *, tm):
    t = idx.shape[0]
    row = lambda i: (i, 0)
    return pl.pallas_call(
        _slots_kernel,
        out_shape=jax.ShapeDtypeStruct((t, TOP_K), I32),
        grid=(t // tm,),
        in_specs=[pl.BlockSpec((tm, TOP_K), row), pl.BlockSpec((tm, TOP_K), row),
                  pl.BlockSpec(start.shape, lambda i: (0, 0))],
        out_specs=pl.BlockSpec((tm, TOP_K), row),
        compiler_params=_params("parallel"),
        name="slots",
    )(idx, rank, start)


def _row_copy(src_ref, src_row, dst_ref, dst_row, sem):
    return pltpu.make_async_copy(src_ref.at[pl.ds(src_row, 1)], dst_ref.at[pl.ds(dst_row, 1)], sem)


def _dispatch_kernel(last_ref, used_ref, nv_ref, pos_ref, x_ref, xs_ref, zbuf, sem, zsem):
    tm = x_ref.shape[0]
    n_tiles = xs_ref.shape[0] // EXPERT_TILE

    def zero_tile(row):
        return pltpu.make_async_copy(zbuf, xs_ref.at[pl.ds(row, EXPERT_TILE)], zsem)

    def for_each_partly_empty_tile(fn):
        def per_expert(e, carry):
            @pl.when(used_ref[e] > 0)
            def _():
                fn(pl.multiple_of(last_ref[e], EXPERT_TILE))
            return carry

        def per_tile(j, carry):
            fn(pl.multiple_of(j * EXPERT_TILE, EXPERT_TILE))
            return carry

        lax.fori_loop(0, N_EXPERTS, per_expert, 0)
        lax.fori_loop(nv_ref[0], n_tiles, per_tile, 0)

    @pl.when(pl.program_id(0) == 0)
    def _():
        zbuf[...] = jnp.zeros_like(zbuf)
        for_each_partly_empty_tile(lambda row: zero_tile(row).start())
        for_each_partly_empty_tile(lambda row: zero_tile(row).wait())

    def body(r, carry):
        for k in range(TOP_K):
            _row_copy(x_ref, r, xs_ref, pos_ref[r * TOP_K + k], sem).start()
        return carry

    lax.fori_loop(0, tm, body, 0)
    n = tm * TOP_K
    pltpu.make_async_copy(xs_ref.at[pl.ds(0, n)], xs_ref.at[pl.ds(0, n)], sem).wait()


def _dispatch(last_row, used, n_valid, pos_flat, h2u, n_rows, *, tm):
    t, dw = h2u.shape
    return pl.pallas_call(
        _dispatch_kernel,
        out_shape=jax.ShapeDtypeStruct((n_rows, dw), F32),
        grid_spec=pltpu.PrefetchScalarGridSpec(
            num_scalar_prefetch=3,
            grid=(t // tm,),
            in_specs=[pl.BlockSpec((tm * TOP_K,), lambda i, *_: (i,), memory_space=pltpu.SMEM),
                      pl.BlockSpec((tm, dw), lambda i, *_: (i, 0))],
            out_specs=pl.BlockSpec(memory_space=pl.ANY),
            scratch_shapes=[pltpu.VMEM((EXPERT_TILE, dw), F32),
                            pltpu.SemaphoreType.DMA, pltpu.SemaphoreType.DMA]),
        compiler_params=_params("arbitrary"),
        name="dispatch",
    )(last_row, used, n_valid, pos_flat, h2u)


def _ffn(x, wg, wu, wd):
    a = jnp.dot(x, wg, preferred_element_type=F32)
    b = jnp.dot(x, wu, preferred_element_type=F32)
    hid = (a * jax.nn.sigmoid(a)) * b
    return jnp.dot(hid.astype(BF16), wd, preferred_element_type=F32)


def _expert_kernel(te_ref, nv_ref, x_ref, wg_ref, wu_ref, wd_ref, o_ref, wg_b, wu_b, wd_b):
    j = pl.program_id(0)
    live = j < nv_ref[0]
    new_expert = (j == 0) | (te_ref[j] != te_ref[jnp.maximum(j - 1, 0)])

    @pl.when(live & new_expert)
    def _():
        wg_b[...] = wg_ref[0].astype(BF16)
        wu_b[...] = wu_ref[0].astype(BF16)
        wd_b[...] = wd_ref[0].astype(BF16)

    @pl.when(live)
    def _():
        o_ref[...] = _ffn(x_ref[...].astype(BF16), wg_b[...], wu_b[...], wd_b[...])

    @pl.when(jnp.logical_not(live))
    def _():
        o_ref[...] = jnp.zeros_like(o_ref)


def _experts(tile_expert, n_valid, xs, wg, wu, wd):
    n_rows, dw = xs.shape
    ne, d, de = wg.shape
    tm = EXPERT_TILE
    row = lambda j, te, nv: (jnp.minimum(j, nv[0] - 1), 0)
    wsel = lambda j, te, nv: (te[j], 0, 0)
    return pl.pallas_call(
        _expert_kernel,
        out_shape=jax.ShapeDtypeStruct((n_rows, dw), F32),
        grid_spec=pltpu.PrefetchScalarGridSpec(
            num_scalar_prefetch=2,
            grid=(n_rows // tm,),
            in_specs=[pl.BlockSpec((tm, dw), row),
                      pl.BlockSpec((1, d, de), wsel), pl.BlockSpec((1, d, de), wsel),
                      pl.BlockSpec((1, de, d), wsel)],
            out_specs=pl.BlockSpec((tm, dw), lambda j, te, nv: (j, 0)),
            scratch_shapes=[pltpu.VMEM((d, de), BF16), pltpu.VMEM((d, de), BF16),
                            pltpu.VMEM((de, d), BF16)]),
        compiler_params=_params("arbitrary"),
        name="experts",
    )(tile_expert, n_valid, xs, wg, wu, wd)


def _combine_kernel(pos_ref, posn_ref, w_ref, h_ref, x1_ref, mod_ref, gpost_ref,
                    sg_ref, su_ref, sd_ref, ys_ref, olat_ref, octx_ref,
                    buf_even, buf_odd, sem_even, sem_odd, *, lat_tiles):
    tm = h_ref.shape[0]
    n = tm * TOP_K
    i = pl.program_id(0)
    last = pl.num_programs(0) - 1

    def drain(buf, sem):
        pltpu.make_async_copy(ys_ref.at[pl.ds(0, n)], buf, sem).wait()

    @pl.when(i == 0)
    def _():
        def body(r, carry):
            for k in range(TOP_K):
                _row_copy(ys_ref, pos_ref[r * TOP_K + k], buf_even, k * tm + r, sem_even).start()
            return carry
        lax.fori_loop(0, tm, body, 0)

    def step(buf, sem, buf_next, sem_next):
        drain(buf, sem)
        for r in range(tm):
            for k in range(TOP_K):
                _row_copy(ys_ref, posn_ref[r * TOP_K + k], buf_next, k * tm + r, sem_next).start()
        acc = _ffn(h_ref[...].astype(BF16), sg_ref[...], su_ref[...], sd_ref[...])
        w = w_ref[...]
        for k in range(TOP_K):
            acc = acc + w[:, k:k + 1] * buf[k * tm:(k + 1) * tm, :]
        gate2 = mod_ref[0, 5:6, :]
        res = x1_ref[...] + gate2 * (_rms(acc) * gpost_ref[...])

        @pl.when(i < lat_tiles)
        def _():
            olat_ref[...] = res

        @pl.when(i >= lat_tiles)
        def _():
            octx_ref[...] = res

        @pl.when(i == last)
        def _():
            drain(buf_next, sem_next)

    @pl.when(i % 2 == 0)
    def _():
        step(buf_even, sem_even, buf_odd, sem_odd)

    @pl.when(i % 2 == 1)
    def _():
        step(buf_odd, sem_odd, buf_even, sem_even)


def _combine(pos_flat, w, h2u, x1, mods3, g_post, sg, su, sd, ys, *, mod_row, tm, n_lat):
    t, d = x1.shape
    dw = h2u.shape[1]
    ds = sg.shape[1]
    steps = t // tm
    lat_tiles = n_lat // tm
    row = lambda i: (i, 0)
    const = lambda i: (0, 0)
    buf = pltpu.VMEM((tm * TOP_K, dw), F32)
    return pl.pallas_call(
        functools.partial(_combine_kernel, lat_tiles=lat_tiles),
        out_shape=(jax.ShapeDtypeStruct((n_lat, d), F32), jax.ShapeDtypeStruct((t - n_lat, d), F32)),
        grid=(steps,),
        in_specs=[pl.BlockSpec((tm * TOP_K,), lambda i: (i,), memory_space=pltpu.SMEM),
                  pl.BlockSpec((tm * TOP_K,), lambda i: (jnp.minimum(i + 1, steps - 1),),
                               memory_space=pltpu.SMEM),
                  pl.BlockSpec((tm, TOP_K), row), pl.BlockSpec((tm, dw), row),
                  pl.BlockSpec((tm, d), row),
                  pl.BlockSpec((1,) + mods3.shape[1:], lambda i: (mod_row(i), 0, 0)),
                  pl.BlockSpec((1, d), const),
                  pl.BlockSpec((d, ds), const), pl.BlockSpec((d, ds), const),
                  pl.BlockSpec((ds, d), const),
                  pl.BlockSpec(memory_space=pl.ANY)],
        out_specs=(pl.BlockSpec((tm, d), lambda i: (jnp.minimum(i, lat_tiles - 1), 0)),
                   pl.BlockSpec((tm, d), lambda i: (jnp.maximum(i - lat_tiles, 0), 0))),
        scratch_shapes=[buf, buf, pltpu.SemaphoreType.DMA, pltpu.SemaphoreType.DMA],
        compiler_params=_params("arbitrary"),
        name="combine",
    )(pos_flat, pos_flat, w, h2u, x1, mods3, g_post, sg, su, sd, ys)


def _rope_tables(n_tokens):
    rows = n_tokens // GRID_W
    row = jnp.repeat(jnp.arange(rows, dtype=F32), GRID_W)
    col = jnp.tile(jnp.arange(GRID_W, dtype=F32), rows)
    n_freq = HEAD_DIM // 4
    freqs = ROPE_THETA ** (-jnp.arange(n_freq, dtype=F32) / n_freq)
    ang_r = row[:, None] * freqs
    ang_c = col[:, None] * freqs
    ang = jnp.concatenate([ang_r, ang_r, ang_c, ang_c], axis=-1)
    sign = jnp.where((jnp.arange(HEAD_DIM) % (HEAD_DIM // 2)) < HEAD_DIM // 4, -1.0, 1.0)
    return jnp.cos(ang), jnp.sin(ang) * sign


def _mixer(x3, mods3, mod_row, wts, rope_tabs, ctx, *, proj_dtype, tm, tq):
    b, n, d = x3.shape
    x = x3.reshape(b * n, d)
    qk_cols = 4 * N_HEADS * HEAD_DIM
    cols = {"cb": qk_cols + N_HEADS * V_HEAD_DIM}
    cols["cc"] = cols["cb"] + D_CONV
    cols["cx"] = cols["cc"] + D_CONV
    cols["ga"] = cols["cx"] + D_CONV
    cols["gc"] = cols["ga"] + d
    proj = _proj(x, mods3, wts["g_pre_mix"], wts["w_in"], mod_row=mod_row(PROJ_TILE[0]),
                 rope_tabs=rope_tabs, out_dtype=proj_dtype, gate_col=cols["ga"], rope_cols=qk_cols,
                 tm=PROJ_TILE[0], tn=PROJ_TILE[1])
    attn = _attention(proj.reshape(b, n, -1), wts["lam_vecs"], wts["g_subln"], ctx, tq=tq)
    merged = _merge(attn.reshape(b * n, -1), proj, wts["conv_w"], wts["w_attn_out"],
                    wts["w_conv_out"], seq_len=n, cols=cols, tm=tm)
    x1, h2u, logits = _post(merged, x, mods3, wts["g_post_mix"], wts["g_pre_ffn"], wts["w_o"],
                            wts["w_router"], mod_row=mod_row(POST_TILE), tm=POST_TILE)
    return x1, h2u, logits, proj


def _moe(x1, h2u, logits, mods3, mod_row, wts, *, n_lat):
    t = x1.shape[0]
    idx, w, rank, counts = _route(logits, wts["router_bias"], tm=ROUTE_TILE)
    tiles_per = (counts[0] + EXPERT_TILE - 1) // EXPERT_TILE
    tile_end = jnp.cumsum(tiles_per)
    start = ((tile_end - tiles_per) * EXPERT_TILE)[None].astype(I32)
    n_tiles = t * TOP_K // EXPERT_TILE + N_EXPERTS
    tile_expert = jnp.minimum(
        jnp.sum(jnp.arange(n_tiles, dtype=I32)[:, None] >= tile_end[None, :], axis=1),
        N_EXPERTS - 1).astype(I32)
    n_valid = tile_end[-1:].astype(I32)
    last_row = ((tile_end - 1) * EXPERT_TILE).astype(I32)
    used = (tiles_per > 0).astype(I32)
    pos = _slots(idx, rank, start, tm=ROUTE_TILE).reshape(t * TOP_K)
    xs = _dispatch(last_row, used, n_valid, pos, h2u, n_tiles * EXPERT_TILE, tm=DISPATCH_TILE)
    ys = _experts(tile_expert, n_valid, xs, wts["w_exp_gate"], wts["w_exp_up"], wts["w_exp_down"])
    return _combine(pos, w, h2u, x1, mods3, wts["g_post_ffn"], wts["w_sh_gate"], wts["w_sh_up"],
                    wts["w_sh_down"], ys, mod_row=mod_row(COMBINE_TILE), tm=COMBINE_TILE,
                    n_lat=n_lat)


def kernel(x_prompt, x_sample, cache_k, cache_v, c, c_ctx, w_ada, b_ada, g_pre_mix, g_post_mix, w_in, lambda_q1, lambda_k1, lambda_q2, lambda_k2, g_subln, conv_w, w_attn_out, w_conv_out, w_o, g_pre_ffn, g_post_ffn, w_router, router_bias, w_exp_gate, w_exp_up, w_exp_down, w_sh_gate, w_sh_up, w_sh_down):
    bp, sp, d = x_prompt.shape
    bs, ss, _ = x_sample.shape
    past = cache_k.shape[2]
    l = 0
    wts = {
        "g_pre_mix": g_pre_mix[l][None], "g_post_mix": g_post_mix[l][None],
        "g_pre_ffn": g_pre_ffn[l][None], "g_post_ffn": g_post_ffn[l][None],
        "g_subln": g_subln[l][None], "conv_w": conv_w[l],
        "lam_vecs": jnp.stack([lambda_q1[l], lambda_k1[l], lambda_q2[l], lambda_k2[l]]),
        "router_bias": router_bias[l][None],
        "w_in": w_in[l].astype(BF16), "w_attn_out": w_attn_out[l].astype(BF16),
        "w_conv_out": w_conv_out[l].astype(BF16), "w_o": w_o[l].astype(BF16),
        "w_router": w_router[l].astype(BF16),
        "w_exp_gate": w_exp_gate[l], "w_exp_up": w_exp_up[l], "w_exp_down": w_exp_down[l],
        "w_sh_gate": w_sh_gate[l].astype(BF16), "w_sh_up": w_sh_up[l].astype(BF16),
        "w_sh_down": w_sh_down[l].astype(BF16),
    }
    pad_rows = 8 - (bs + 1)
    c_rows = jnp.concatenate([c, c_ctx[None], jnp.zeros((pad_rows, d), F32)], axis=0)
    mods3 = _ada(c_rows, w_ada[l], b_ada[l][None]).reshape(8, 6, d)

    tm = ROW_TILE
    ctx_row = lambda tm_: (lambda i: bs)
    lat_row = lambda tm_: (lambda i: (i * tm_) // ss)
    all_row = lambda tm_: (lambda i: jnp.minimum((i * tm_) // ss, bs))
    x1p, h2p, lgp, proj_p = _mixer(x_prompt, mods3, ctx_row, wts, None, None,
                                   proj_dtype=F32, tm=tm, tq=sp)
    hk = 2 * N_HEADS * HEAD_DIM
    new_k = proj_p[:, hk:2 * hk].reshape(bp, 1, sp, 2, N_HEADS, HEAD_DIM)
    new_v = proj_p[:, 2 * hk:2 * hk + N_HEADS * V_HEAD_DIM].reshape(bp, 1, sp, N_HEADS, V_HEAD_DIM)

    ctx = (cache_k[:, l].reshape(bs, past, hk).astype(BF16),
           cache_v[:, l].reshape(bs, past, N_HEADS * V_HEAD_DIM).astype(BF16))
    x1s, h2s, lgs, _ = _mixer(x_sample, mods3, lat_row, wts, _rope_tables(ss), ctx,
                              proj_dtype=BF16, tm=tm, tq=QUERY_TILE)

    n_lat = bs * ss
    ys, yp = _moe(jnp.concatenate([x1s, x1p]), jnp.concatenate([h2s, h2p]),
                  jnp.concatenate([lgs, lgp]), mods3, all_row, wts, n_lat=n_lat)
    return (yp.reshape(bp, sp, d), ys.reshape(bs, ss, d), new_k, new_v)
```

```python
import functools
import math

import jax
import jax.numpy as jnp
from jax import lax
from jax.experimental import pallas as pl
from jax.experimental.pallas import tpu as pltpu

F32 = jnp.float32
BF16 = jnp.bfloat16
I32 = jnp.int32

EPS = 1e-6
N_HEADS = 8
HEAD_DIM = 128
V_HEAD_DIM = 2 * HEAD_DIM
D_CONV = 1024
N_EXPERTS = 64
TOP_K = 8
N_GROUPS = 8
TOPK_GROUPS = 4
ROUTED_SCALE = 2.5
ROPE_THETA = 10000.0
GRID_W = 64
LAM_INIT = 0.8 - 0.6 * math.exp(-0.3 * 0)

LANES = 128
HALO_ROWS = 16
VMEM_LIMIT_BYTES = 56 * 1024 * 1024
EXPERT_TILE = 256
KEY_CHUNK = 512
QUERY_TILE = 256
PROJ_TILE = (1024, 1024)
ROW_TILE = 512
POST_TILE = 256
SMEM_TILES = 4
COMBINE_TILE = 128
ROUTE_TILE = 512


def _params(*sem):
    return pltpu.CompilerParams(dimension_semantics=sem, vmem_limit_bytes=VMEM_LIMIT_BYTES)


def _rms(x):
    return x * lax.rsqrt(jnp.mean(x * x, axis=-1, keepdims=True) + EPS)


def _ada_kernel(c_ref, w_ref, b_ref, o_ref):
    c = c_ref[...]
    a = (c * jax.nn.sigmoid(c)).astype(BF16)
    o_ref[...] = jnp.dot(a, w_ref[...].astype(BF16), preferred_element_type=F32) + b_ref[...]


def _ada(c_rows, w_ada, b_ada, tn=1024):
    r, d = c_rows.shape
    n = w_ada.shape[1]
    return pl.pallas_call(
        _ada_kernel,
        out_shape=jax.ShapeDtypeStruct((r, n), F32),
        grid=(n // tn,),
        in_specs=[pl.BlockSpec((r, d), lambda j: (0, 0)),
                  pl.BlockSpec((d, tn), lambda j: (0, j)),
                  pl.BlockSpec((1, tn), lambda j: (0, j))],
        out_specs=pl.BlockSpec((r, tn), lambda j: (0, j)),
        compiler_params=_params("arbitrary"),
        name="ada",
    )(c_rows, w_ada, b_ada)


def _proj_kernel(*refs, rope, n_rope_tiles, sig_start):
    if rope:
        x_ref, mod_ref, g_ref, w_ref, cos_ref, sin_ref, o_ref, h_scr = refs
    else:
        x_ref, mod_ref, g_ref, w_ref, o_ref, h_scr = refs
    j = pl.program_id(1)

    @pl.when(j == 0)
    def _():
        shift = mod_ref[0, 0:1, :]
        scale = mod_ref[0, 1:2, :]
        h = (_rms(x_ref[...]) * g_ref[...]) * (1 + scale) + shift
        h_scr[...] = h.astype(BF16)

    acc = jnp.dot(h_scr[...], w_ref[...], preferred_element_type=F32)
    tn = acc.shape[1]

    if rope:
        @pl.when(j < n_rope_tiles)
        def _():
            cos = cos_ref[...]
            sin = sin_ref[...]
            lane = lax.broadcasted_iota(jnp.int32, cos.shape, 1)
            first = (lane % (HEAD_DIM // 2)) < (HEAD_DIM // 4)
            for s in range(tn // HEAD_DIM):
                xs = acc[:, s * HEAD_DIM:(s + 1) * HEAD_DIM]
                partner = jnp.where(first,
                                    pltpu.roll(xs, HEAD_DIM - HEAD_DIM // 4, axis=1),
                                    pltpu.roll(xs, HEAD_DIM // 4, axis=1))
                o_ref[:, s * HEAD_DIM:(s + 1) * HEAD_DIM] = (xs * cos + partner * sin).astype(o_ref.dtype)
        plain_lo = n_rope_tiles
    else:
        plain_lo = 0

    @pl.when((j >= plain_lo) & (j < sig_start))
    def _():
        o_ref[...] = acc.astype(o_ref.dtype)

    @pl.when(j >= sig_start)
    def _():
        o_ref[...] = jax.nn.sigmoid(acc).astype(o_ref.dtype)


def _proj(x, mods3, g, w, *, mod_row, rope_tabs, out_dtype, gate_col, rope_cols, tm, tn=512):
    t, d = x.shape
    n = w.shape[1]
    rope = rope_tabs is not None
    in_specs = [pl.BlockSpec((tm, d), lambda i, j: (i, 0)),
                pl.BlockSpec((1,) + mods3.shape[1:], lambda i, j: (mod_row(i), 0, 0)),
                pl.BlockSpec((1, d), lambda i, j: (0, 0)),
                pl.BlockSpec((d, tn), lambda i, j: (0, j))]
    args = [x, mods3, g, w]
    if rope:
        cos, sin = rope_tabs
        nblk = cos.shape[0] // tm
        in_specs += [pl.BlockSpec((tm, HEAD_DIM), lambda i, j: (i % nblk, 0)),
                     pl.BlockSpec((tm, HEAD_DIM), lambda i, j: (i % nblk, 0))]
        args += [cos, sin]
    kern = functools.partial(_proj_kernel, rope=rope, n_rope_tiles=rope_cols // tn,
                             sig_start=gate_col // tn)
    return pl.pallas_call(
        kern,
        out_shape=jax.ShapeDtypeStruct((t, n), out_dtype),
        grid=(t // tm, n // tn),
        in_specs=in_specs,
        out_specs=pl.BlockSpec((tm, tn), lambda i, j: (i, j)),
        scratch_shapes=[pltpu.VMEM((tm, d), BF16)],
        compiler_params=_params("parallel", "arbitrary"),
        name="proj_rope" if rope else "proj",
    )(*args)


def _attn_kernel(*refs, has_ctx):
    if has_ctx:
        (lam_ref, gs_ref, q1_ref, q2_ref, k1_ref, k2_ref, v_ref,
         k1c_ref, k2c_ref, vc_ref, o_ref, s_even, s_odd, m_even, m_odd) = refs
    else:
        (lam_ref, gs_ref, q1_ref, q2_ref, k1_ref, k2_ref, v_ref,
         o_ref, s_even, s_odd, m_even, m_odd) = refs
    t = pl.program_id(1)
    lv = lam_ref[...]
    lam = (jnp.exp(jnp.sum(lv[0:1] * lv[1:2], axis=-1, keepdims=True))
           - jnp.exp(jnp.sum(lv[2:3] * lv[3:4], axis=-1, keepdims=True)) + LAM_INIT)
    c = (HEAD_DIM ** -0.5) * math.log2(math.e)
    nt = (((1,), (1,)), ((), ()))
    tq = q1_ref.shape[1]

    chunks = []
    off = 0
    groups = [(k1_ref, k2_ref, v_ref)] + ([(k1c_ref, k2c_ref, vc_ref)] if has_ctx else [])
    for ka, kb, vv in groups:
        n = ka.shape[1]
        ch = min(KEY_CHUNK, n)
        for c0 in range(0, n, ch):
            chunks.append((ka, kb, vv, c0, ch, off + c0))
        off += n

    @pl.when(t == 0)
    def _():
        s_odd[...] = jnp.zeros_like(s_odd)
        m_odd[...] = jnp.zeros_like(m_odd)

    def step(s_fill, m_fill, s_drain, m_drain):
        qs = (q1_ref[0].astype(BF16), q2_ref[0].astype(BF16))
        mbs = (m_drain[0], m_drain[1])
        mrun = [jnp.full((tq, LANES), -jnp.inf, F32)] * 2
        lrun = [jnp.zeros((tq, LANES), F32)] * 2
        acc = [jnp.zeros((tq, V_HEAD_DIM), F32)] * 2
        for ka, kb, vv, c0, ch, col in chunks:
            for mp in range(2):
                k = (ka, kb)[mp][0, c0:c0 + ch, :].astype(BF16)
                s = lax.dot_general(qs[mp], k, nt, preferred_element_type=F32)
                s_fill[mp, :, col:col + ch] = s
                for g in range(ch // LANES):
                    mrun[mp] = jnp.maximum(mrun[mp], s[:, g * LANES:(g + 1) * LANES])
            v = vv[0, c0:c0 + ch, :].astype(BF16)
            for mp in range(2):
                ps = []
                for g in range(ch // LANES):
                    sg = s_drain[mp, :, col + g * LANES:col + (g + 1) * LANES]
                    pg = jnp.exp2((sg - mbs[mp]) * c)
                    lrun[mp] = lrun[mp] + pg
                    ps.append(pg.astype(BF16))
                acc[mp] = acc[mp] + jnp.dot(jnp.concatenate(ps, axis=1), v, preferred_element_type=F32)
        for mp in range(2):
            m_fill[mp] = jnp.broadcast_to(jnp.max(mrun[mp], axis=-1, keepdims=True), (tq, LANES))
        l1 = jnp.sum(lrun[0], axis=-1, keepdims=True)
        l2 = jnp.sum(lrun[1], axis=-1, keepdims=True)
        o = acc[0] * (1.0 / l1) - acc[1] * (lam / l2)
        o_ref[0] = ((_rms(o) * gs_ref[...]) * (1 - LAM_INIT)).astype(o_ref.dtype)

    @pl.when(t % 2 == 0)
    def _():
        step(s_even, m_even, s_odd, m_odd)

    @pl.when(t % 2 == 1)
    def _():
        step(s_odd, m_odd, s_even, m_even)


def _attention(proj3, lam_vecs, g_subln, ctx, *, tq):
    b, n, _ = proj3.shape
    h = N_HEADS
    nq = n // tq
    n_tiles = h * nq
    has_ctx = ctx is not None
    f_head = lambda t: jnp.minimum(t, n_tiles - 1) // nq
    f_row = lambda t: jnp.minimum(t, n_tiles - 1) % nq
    d_head = lambda t: jnp.maximum(t - 1, 0) // nq
    d_row = lambda t: jnp.maximum(t - 1, 0) % nq
    q_spec = lambda off: pl.BlockSpec((1, tq, HEAD_DIM), lambda bi, t: (bi, f_row(t), off + f_head(t)))
    k_spec = lambda rows, off: pl.BlockSpec((1, rows, HEAD_DIM), lambda bi, t: (bi, 0, off + f_head(t)))
    v_spec = lambda rows, off: pl.BlockSpec((1, rows, V_HEAD_DIM), lambda bi, t: (bi, 0, off + d_head(t)))
    in_specs = [pl.BlockSpec((4, HEAD_DIM), lambda bi, t: (0, 0)),
                pl.BlockSpec((1, V_HEAD_DIM), lambda bi, t: (0, 0)),
                q_spec(0), q_spec(h), k_spec(n, 2 * h), k_spec(n, 3 * h), v_spec(n, 2 * h)]
    args = [lam_vecs, g_subln, proj3, proj3, proj3, proj3, proj3]
    n_keys = n
    if has_ctx:
        ck, cv = ctx
        p = ck.shape[1]
        n_keys += p
        in_specs += [k_spec(p, 0), k_spec(p, h), v_spec(p, 0)]
        args += [ck, ck, cv]
    scores = pltpu.VMEM((2, tq, n_keys), F32)
    maxima = pltpu.VMEM((2, tq, LANES), F32)
    return pl.pallas_call(
        functools.partial(_attn_kernel, has_ctx=has_ctx),
        out_shape=jax.ShapeDtypeStruct((b, n, h * V_HEAD_DIM), BF16),
        grid=(b, n_tiles + 1),
        in_specs=in_specs,
        out_specs=pl.BlockSpec((1, tq, V_HEAD_DIM), lambda bi, t: (bi, d_row(t), d_head(t))),
        scratch_shapes=[scores, scores, maxima, maxima],
        compiler_params=_params("parallel", "arbitrary"),
        name="attn_ctx" if has_ctx else "attn",
    )(*args)


def _merge_kernel(attn_ref, cb_ref, cc_ref, cx_ref, ccp_ref, cxp_ref, ccn_ref, cxn_ref,
                  cw_ref, wa_ref, wc_ref, ga_ref, gc_ref, o_ref, conv_scr, *, seq_len):
    i = pl.program_id(0)
    j = pl.program_id(1)
    tm = cc_ref.shape[0]

    @pl.when(j == 0)
    def _():
        u = cc_ref[...].astype(F32) * cx_ref[...].astype(F32)
        u_before = (ccp_ref[...].astype(F32) * cxp_ref[...].astype(F32))[HALO_ROWS - 1:HALO_ROWS, :]
        u_after = (ccn_ref[...].astype(F32) * cxn_ref[...].astype(F32))[0:1, :]
        row = lax.broadcasted_iota(jnp.int32, (tm, 1), 0)
        pos = (i * tm + row) % seq_len
        u_prev = jnp.where(row == 0, u_before, pltpu.roll(u, 1, axis=0))
        u_prev = jnp.where(pos == 0, 0.0, u_prev)
        u_next = jnp.where(row == tm - 1, u_after, pltpu.roll(u, tm - 1, axis=0))
        u_next = jnp.where(pos == seq_len - 1, 0.0, u_next)
        cw = cw_ref[...]
        conv = u_prev * cw[0:1, :] + u * cw[1:2, :] + u_next * cw[2:3, :]
        conv_scr[...] = (cb_ref[...].astype(F32) * conv).astype(BF16)

    ya = jnp.dot(attn_ref[...], wa_ref[...], preferred_element_type=F32)
    yc = jnp.dot(conv_scr[...], wc_ref[...], preferred_element_type=F32)
    o_ref[...] = (ga_ref[...].astype(F32) * ya + gc_ref[...].astype(F32) * yc).astype(o_ref.dtype)


def _merge(attn, proj, conv_w, w_attn_out, w_conv_out, *, seq_len, cols, tm, tn=1024):
    t, aw = attn.shape
    d = w_attn_out.shape[1]
    cb0, cc0, cx0, ga0, gc0 = (cols[k] for k in ("cb", "cc", "cx", "ga", "gc"))
    hb = tm // HALO_ROWS
    last_halo = t // HALO_ROWS - 1
    cblk = lambda c0: pl.BlockSpec((tm, D_CONV), lambda i, j: (i, c0 // D_CONV))
    halo_prev = lambda c0: pl.BlockSpec(
        (HALO_ROWS, D_CONV), lambda i, j: (jnp.maximum(i * hb - 1, 0), c0 // D_CONV))
    halo_next = lambda c0: pl.BlockSpec(
        (HALO_ROWS, D_CONV), lambda i, j: (jnp.minimum((i + 1) * hb, last_halo), c0 // D_CONV))
    in_specs = [pl.BlockSpec((tm, aw), lambda i, j: (i, 0)),
                cblk(cb0), cblk(cc0), cblk(cx0),
                halo_prev(cc0), halo_prev(cx0), halo_next(cc0), halo_next(cx0),
                pl.BlockSpec(conv_w.shape, lambda i, j: (0, 0)),
                pl.BlockSpec((aw, tn), lambda i, j: (0, j)),
                pl.BlockSpec((D_CONV, tn), lambda i, j: (0, j)),
                pl.BlockSpec((tm, tn), lambda i, j: (i, ga0 // tn + j)),
                pl.BlockSpec((tm, tn), lambda i, j: (i, gc0 // tn + j))]
    return pl.pallas_call(
        functools.partial(_merge_kernel, seq_len=seq_len),
        out_shape=jax.ShapeDtypeStruct((t, d), BF16),
        grid=(t // tm, d // tn),
        in_specs=in_specs,
        out_specs=pl.BlockSpec((tm, tn), lambda i, j: (i, j)),
        scratch_shapes=[pltpu.VMEM((tm, D_CONV), BF16)],
        compiler_params=_params("parallel", "arbitrary"),
        name="merge",
    )(attn, proj, proj, proj, proj, proj, proj, proj, conv_w, w_attn_out, w_conv_out, proj, proj)


def _post_kernel(m_ref, x_ref, mod_ref, gpost_ref, gpre_ref, wo_ref, wr_ref,
                 x1_ref, h2_ref, lg_ref):
    y = jnp.dot(m_ref[...], wo_ref[...], preferred_element_type=F32)
    gate1 = mod_ref[0, 2:3, :]
    shift2 = mod_ref[0, 3:4, :]
    scale2 = mod_ref[0, 4:5, :]
    x1 = x_ref[...] + gate1 * (_rms(y) * gpost_ref[...])
    x1_ref[...] = x1
    h2 = (_rms(x1) * gpre_ref[...]) * (1 + scale2) + shift2
    h2_ref[...] = h2
    lg_ref[...] = jnp.dot(h2.astype(BF16), wr_ref[...], preferred_element_type=F32)


def _post(merged, x, mods3, g_post, g_pre, w_o, w_router, *, mod_row, tm):
    t, d = x.shape
    e = w_router.shape[1]
    row = lambda i: (i, 0)
    const = lambda i: (0, 0)
    return pl.pallas_call(
        _post_kernel,
        out_shape=(jax.ShapeDtypeStruct((t, d), F32),
                   jax.ShapeDtypeStruct((t, d), F32),
                   jax.ShapeDtypeStruct((t, e), F32)),
        grid=(t // tm,),
        in_specs=[pl.BlockSpec((tm, d), row), pl.BlockSpec((tm, d), row),
                  pl.BlockSpec((1,) + mods3.shape[1:], lambda i: (mod_row(i), 0, 0)),
                  pl.BlockSpec((1, d), const), pl.BlockSpec((1, d), const),
                  pl.BlockSpec((d, d), const), pl.BlockSpec((d, e), const)],
        out_specs=(pl.BlockSpec((tm, d), row), pl.BlockSpec((tm, d), row),
                   pl.BlockSpec((tm, e), row)),
        compiler_params=_params("parallel"),
        name="post",
    )(merged, x, mods3, g_post, g_pre, w_o, w_router)


def _route_kernel(lg_ref, bias_ref, idx_ref, w_ref, rank_ref, cnt_ref, carry_scr):
    @pl.when(pl.program_id(0) == 0)
    def _():
        carry_scr[...] = jnp.zeros_like(carry_scr)

    s = jax.nn.sigmoid(lg_ref[...])
    biased = s + bias_ref[...]
    tm, e = s.shape
    per = e // N_GROUPS
    lane = lax.broadcasted_iota(I32, (tm, e), 1)
    grp = lane // per
    neg = -jnp.inf

    def first_argmax(v):
        m = jnp.max(v, axis=-1, keepdims=True)
        idx = jnp.min(jnp.where(v == m, lane, e), axis=-1, keepdims=True)
        return m, idx

    scores = []
    for g in range(N_GROUPS):
        vg = jnp.where(grp == g, biased, neg)
        m1, i1 = first_argmax(vg)
        m2 = jnp.max(jnp.where(lane == i1, neg, vg), axis=-1, keepdims=True)
        scores.append(m1 + m2)
    allowed = jnp.zeros((tm, e), jnp.bool_)
    for g in range(N_GROUPS):
        beaten_by = jnp.zeros((tm, 1), I32)
        for g2 in range(N_GROUPS):
            if g2 == g:
                continue
            wins = (scores[g2] > scores[g]) | ((scores[g2] == scores[g]) & (g2 < g))
            beaten_by = beaten_by + wins.astype(I32)
        allowed = allowed | ((grp == g) & (beaten_by < TOPK_GROUPS))
    masked = jnp.where(allowed, biased, neg)
    hits, ids = [], []
    for _ in range(TOP_K):
        _, idx = first_argmax(masked)
        hit = lane == idx
        hits.append(hit)
        ids.append(idx)
        masked = jnp.where(hit, neg, masked)
    chosen = functools.reduce(jnp.logical_or, hits)
    w = jnp.where(chosen, s, 0.0)
    w = w / jnp.sum(w, axis=-1, keepdims=True) * ROUTED_SCALE

    onehot = jnp.where(chosen, 1.0, 0.0)
    r_i = lax.broadcasted_iota(I32, (tm, tm), 0)
    c_i = lax.broadcasted_iota(I32, (tm, tm), 1)
    below = jnp.where(c_i < r_i, 1.0, 0.0).astype(BF16)
    rank = jnp.dot(below, onehot.astype(BF16), preferred_element_type=F32) + carry_scr[...]
    carry = carry_scr[...] + jnp.sum(onehot, axis=0, keepdims=True)
    carry_scr[...] = carry
    cnt_ref[...] = carry.astype(I32)

    col = lax.broadcasted_iota(I32, (tm, TOP_K), 1)
    idx_o = jnp.zeros((tm, TOP_K), I32)
    w_o = jnp.zeros((tm, TOP_K), F32)
    rank_o = jnp.zeros((tm, TOP_K), F32)
    for k in range(TOP_K):
        wk = jnp.sum(jnp.where(hits[k], w, 0.0), axis=-1, keepdims=True)
        rk = jnp.sum(jnp.where(hits[k], rank, 0.0), axis=-1, keepdims=True)
        idx_o = jnp.where(col == k, ids[k], idx_o)
        w_o = jnp.where(col == k, wk, w_o)
        rank_o = jnp.where(col == k, rk, rank_o)
    idx_ref[...] = idx_o
    w_ref[...] = w_o
    rank_ref[...] = rank_o.astype(I32)


def _route(logits, bias, *, tm):
    t, e = logits.shape
    row = lambda i: (i, 0)
    return pl.pallas_call(
        _route_kernel,
        out_shape=(jax.ShapeDtypeStruct((t, TOP_K), I32), jax.ShapeDtypeStruct((t, TOP_K), F32),
                   jax.ShapeDtypeStruct((t, TOP_K), I32), jax.ShapeDtypeStruct((1, e), I32)),
        grid=(t // tm,),
        in_specs=[pl.BlockSpec((tm, e), row), pl.BlockSpec((1, e), lambda i: (0, 0))],
        out_specs=(pl.BlockSpec((tm, TOP_K), row), pl.BlockSpec((tm, TOP_K), row),
                   pl.BlockSpec((tm, TOP_K), row), pl.BlockSpec((1, e), lambda i: (0, 0))),
        scratch_shapes=[pltpu.VMEM((1, e), F32)],
        compiler_params=_params("arbitrary"),
        name="route",
    )(logits, bias)


def _slots_kernel(idx_ref, rank_ref, start_ref, o_ref):
    idx = idx_ref[...]
    tm = idx.shape[0]
    e = start_ref.shape[1]
    lane = lax.broadcasted_iota(I32, (tm, e), 1)
    col = lax.broadcasted_iota(I32, (tm, TOP_K), 1)
    base = jnp.zeros((tm, TOP_K), I32)
    for k in range(TOP_K):
        sk = jnp.sum(jnp.where(lane == idx[:, k:k + 1], start_ref[...], 0), axis=-1, keepdims=True)
        base = jnp.where(col == k, sk, base)
    o_ref[...] = base + rank_ref[...]


def _slots(idx, rank, start, *, tm):
    t = idx.shape[0]
    row = lambda i: (i, 0)
    return pl.pallas_call(
        _slots_kernel,
        out_shape=jax.ShapeDtypeStruct((t, TOP_K), I32),
        grid=(t // tm,),
        in_specs=[pl.BlockSpec((tm, TOP_K), row), pl.BlockSpec((tm, TOP_K), row),
                  pl.BlockSpec(start.shape, lambda i: (0, 0))],
        out_specs=pl.BlockSpec((tm, TOP_K), row),
        compiler_params=_params("parallel"),
        name="slots",
    )(idx, rank, start)


def _row_copy(src_ref, src_row, dst_ref, dst_row, sem):
    return pltpu.make_async_copy(src_ref.at[pl.ds(src_row, 1)], dst_ref.at[pl.ds(dst_row, 1)], sem)


def _ffn(x, wg, wu, wd):
    a = jnp.dot(x, wg, preferred_element_type=F32)
    b = jnp.dot(x, wu, preferred_element_type=F32)
    hid = (a * jax.nn.sigmoid(a)) * b
    return jnp.dot(hid.astype(BF16), wd, preferred_element_type=F32)


def _expert_kernel(te_ref, nv_ref, tok_ref, slot_ref, h_ref, wg_ref, wu_ref, wd_ref, y_ref,
                   wg_b, wu_b, wd_b, x_even, x_odd, y_even, y_odd,
                   gsem_even, gsem_odd, ssem_even, ssem_odd, *, n_slots):
    tm = EXPERT_TILE
    j = pl.program_id(0)
    nv = nv_ref[0]
    n_tiles = pl.num_programs(0) - 1
    tok_off = (jnp.minimum(j + 1, n_tiles - 1) % SMEM_TILES) * tm
    slot_off = (jnp.maximum(j - 1, 0) % SMEM_TILES) * tm

    def gather_wait(xbuf, sem):
        pltpu.make_async_copy(h_ref.at[pl.ds(0, tm)], xbuf, sem).wait()

    def scatter_wait(ybuf, sem):
        pltpu.make_async_copy(ybuf, y_ref.at[pl.ds(0, tm)], sem).wait()

    def gather_next(xbuf, sem):
        for r in range(tm):
            _row_copy(h_ref, tok_ref[tok_off + r], xbuf, r, sem).start()

    def scatter_prev(ybuf, sem):
        for r in range(tm):
            _row_copy(ybuf, r, y_ref, slot_ref[slot_off + r], sem).start()

    jc = jnp.minimum(j, n_tiles - 1)

    @pl.when((j < nv) & ((j == 0) | (te_ref[jc] != te_ref[jnp.maximum(jc - 1, 0)])))
    def _():
        wg_b[...] = wg_ref[0].astype(BF16)
        wu_b[...] = wu_ref[0].astype(BF16)
        wd_b[...] = wd_ref[0].astype(BF16)

    @pl.when(j == 0)
    def _():
        def body(r, carry):
            _row_copy(h_ref, tok_ref[r], x_even, r, gsem_even).start()
            return carry
        lax.fori_loop(0, tm, body, 0)

    def step(x_cur, y_cur, gsem_cur, ssem_cur, x_nxt, y_prv, gsem_nxt, ssem_prv, first):
        gather_wait(x_cur, gsem_cur)
        if not first:
            @pl.when(j >= 2)
            def _():
                scatter_wait(y_cur, ssem_cur)
        gather_next(x_nxt, gsem_nxt)
        if not first:
            scatter_prev(y_prv, ssem_prv)
        y_cur[...] = _ffn(x_cur[...].astype(BF16), wg_b[...], wu_b[...], wd_b[...])

    @pl.when(j == 0)
    def _():
        step(x_even, y_even, gsem_even, ssem_even, x_odd, y_odd, gsem_odd, ssem_odd, True)

    @pl.when((j > 0) & (j < nv) & (j % 2 == 0))
    def _():
        step(x_even, y_even, gsem_even, ssem_even, x_odd, y_odd, gsem_odd, ssem_odd, False)

    @pl.when((j < nv) & (j % 2 == 1))
    def _():
        step(x_odd, y_odd, gsem_odd, ssem_odd, x_even, y_even, gsem_even, ssem_even, False)

    def finish(x_spare, y_last, gsem_spare, ssem_last, y_other, ssem_other):
        gather_wait(x_spare, gsem_spare)
        @pl.when(nv >= 2)
        def _():
            scatter_wait(y_other, ssem_other)
        def body(r, carry):
            _row_copy(y_last, r, y_ref, slot_ref[slot_off + r], ssem_last).start()
            return carry
        lax.fori_loop(0, tm, body, 0)
        scatter_wait(y_last, ssem_last)
        x_spare[...] = jnp.zeros_like(x_spare)
        pad = pltpu.make_async_copy(x_spare, y_ref.at[pl.ds(n_slots, tm)], gsem_spare)
        pad.start()
        pad.wait()

    @pl.when((j == nv) & (j % 2 == 0))
    def _():
        finish(x_even, y_odd, gsem_even, ssem_odd, y_even, ssem_even)

    @pl.when((j == nv) & (j % 2 == 1))
    def _():
        finish(x_odd, y_even, gsem_odd, ssem_even, y_odd, ssem_odd)


def _experts(tile_expert, n_valid, tok, slot, h2, wg, wu, wd, *, n_slots):
    ne, d, de = wg.shape
    tm = EXPERT_TILE
    n_tiles = tok.shape[0] // tm
    blk = SMEM_TILES * tm
    wsel = lambda j, te, nv: (te[jnp.minimum(j, n_tiles - 1)], 0, 0)
    xbuf = pltpu.VMEM((tm, d), F32)
    return pl.pallas_call(
        functools.partial(_expert_kernel, n_slots=n_slots),
        out_shape=jax.ShapeDtypeStruct((n_slots + tm, d), F32),
        grid_spec=pltpu.PrefetchScalarGridSpec(
            num_scalar_prefetch=2,
            grid=(n_tiles + 1,),
            in_specs=[pl.BlockSpec((blk,), lambda j, te, nv: (jnp.minimum(j + 1, n_tiles - 1) // SMEM_TILES,),
                                   memory_space=pltpu.SMEM),
                      pl.BlockSpec((blk,), lambda j, te, nv: (jnp.maximum(j - 1, 0) // SMEM_TILES,),
                                   memory_space=pltpu.SMEM),
                      pl.BlockSpec(memory_space=pl.ANY),
                      pl.BlockSpec((1, d, de), wsel), pl.BlockSpec((1, d, de), wsel),
                      pl.BlockSpec((1, de, d), wsel)],
            out_specs=pl.BlockSpec(memory_space=pl.ANY),
            scratch_shapes=[pltpu.VMEM((d, de), BF16), pltpu.VMEM((d, de), BF16),
                            pltpu.VMEM((de, d), BF16), xbuf, xbuf, xbuf, xbuf,
                            pltpu.SemaphoreType.DMA, pltpu.SemaphoreType.DMA,
                            pltpu.SemaphoreType.DMA, pltpu.SemaphoreType.DMA]),
        compiler_params=_params("arbitrary"),
        name="experts",
    )(tile_expert, n_valid, tok, slot, h2, wg, wu, wd)


def _combine_kernel(*refs, lat_tiles):
    w_ref, h_ref, x1_ref, mod_ref, gpost_ref, sg_ref, su_ref, sd_ref = refs[:8]
    y_refs = refs[8:8 + TOP_K]
    olat_ref, octx_ref = refs[8 + TOP_K:]
    i = pl.program_id(0)
    acc = _ffn(h_ref[...].astype(BF16), sg_ref[...], su_ref[...], sd_ref[...])
    w = w_ref[...]
    for k in range(TOP_K):
        acc = acc + w[:, k:k + 1] * y_refs[k][...]
    gate2 = mod_ref[0, 5:6, :]
    res = x1_ref[...] + gate2 * (_rms(acc) * gpost_ref[...])

    @pl.when(i < lat_tiles)
    def _():
        olat_ref[...] = res

    @pl.when(i >= lat_tiles)
    def _():
        octx_ref[...] = res


def _combine(w, h2, x1, mods3, g_post, sg, su, sd, ys, *, mod_row, tm, n_lat):
    t, d = x1.shape
    ds = sg.shape[1]
    steps = t // tm
    lat_tiles = n_lat // tm
    row = lambda i: (i, 0)
    const = lambda i: (0, 0)
    choice = lambda k: pl.BlockSpec((tm, d), lambda i: (k * steps + i, 0))
    return pl.pallas_call(
        functools.partial(_combine_kernel, lat_tiles=lat_tiles),
        out_shape=(jax.ShapeDtypeStruct((n_lat, d), F32), jax.ShapeDtypeStruct((t - n_lat, d), F32)),
        grid=(steps,),
        in_specs=[pl.BlockSpec((tm, TOP_K), row), pl.BlockSpec((tm, d), row),
                  pl.BlockSpec((tm, d), row),
                  pl.BlockSpec((1,) + mods3.shape[1:], lambda i: (mod_row(i), 0, 0)),
                  pl.BlockSpec((1, d), const),
                  pl.BlockSpec((d, ds), const), pl.BlockSpec((d, ds), const),
                  pl.BlockSpec((ds, d), const)] + [choice(k) for k in range(TOP_K)],
        out_specs=(pl.BlockSpec((tm, d), lambda i: (jnp.minimum(i, lat_tiles - 1), 0)),
                   pl.BlockSpec((tm, d), lambda i: (jnp.maximum(i - lat_tiles, 0), 0))),
        compiler_params=_params("arbitrary"),
        name="combine",
    )(w, h2, x1, mods3, g_post, sg, su, sd, *([ys] * TOP_K))


def _rope_tables(n_tokens):
    rows = n_tokens // GRID_W
    row = jnp.repeat(jnp.arange(rows, dtype=F32), GRID_W)
    col = jnp.tile(jnp.arange(GRID_W, dtype=F32), rows)
    n_freq = HEAD_DIM // 4
    freqs = ROPE_THETA ** (-jnp.arange(n_freq, dtype=F32) / n_freq)
    ang_r = row[:, None] * freqs
    ang_c = col[:, None] * freqs
    ang = jnp.concatenate([ang_r, ang_r, ang_c, ang_c], axis=-1)
    sign = jnp.where((jnp.arange(HEAD_DIM) % (HEAD_DIM // 2)) < HEAD_DIM // 4, -1.0, 1.0)
    return jnp.cos(ang), jnp.sin(ang) * sign


def _mixer(x3, mods3, mod_row, wts, rope_tabs, ctx, *, proj_dtype, tm, tq):
    b, n, d = x3.shape
    x = x3.reshape(b * n, d)
    qk_cols = 4 * N_HEADS * HEAD_DIM
    cols = {"cb": qk_cols + N_HEADS * V_HEAD_DIM}
    cols["cc"] = cols["cb"] + D_CONV
    cols["cx"] = cols["cc"] + D_CONV
    cols["ga"] = cols["cx"] + D_CONV
    cols["gc"] = cols["ga"] + d
    proj = _proj(x, mods3, wts["g_pre_mix"], wts["w_in"], mod_row=mod_row(PROJ_TILE[0]),
                 rope_tabs=rope_tabs, out_dtype=proj_dtype, gate_col=cols["ga"], rope_cols=qk_cols,
                 tm=PROJ_TILE[0], tn=PROJ_TILE[1])
    attn = _attention(proj.reshape(b, n, -1), wts["lam_vecs"], wts["g_subln"], ctx, tq=tq)
    merged = _merge(attn.reshape(b * n, -1), proj, wts["conv_w"], wts["w_attn_out"],
                    wts["w_conv_out"], seq_len=n, cols=cols, tm=tm)
    x1, h2u, logits = _post(merged, x, mods3, wts["g_post_mix"], wts["g_pre_ffn"], wts["w_o"],
                            wts["w_router"], mod_row=mod_row(POST_TILE), tm=POST_TILE)
    return x1, h2u, logits, proj


def _moe(x1, h2u, logits, mods3, mod_row, wts, *, n_lat):
    t = x1.shape[0]
    idx, w, rank, counts = _route(logits, wts["router_bias"], tm=ROUTE_TILE)
    tiles_per = (counts[0] + EXPERT_TILE - 1) // EXPERT_TILE
    tile_end = jnp.cumsum(tiles_per)
    start = ((tile_end - tiles_per) * EXPERT_TILE)[None].astype(I32)
    n_tiles = t * TOP_K // EXPERT_TILE + N_EXPERTS
    tile_expert = jnp.minimum(
        jnp.sum(jnp.arange(n_tiles, dtype=I32)[:, None] >= tile_end[None, :], axis=1),
        N_EXPERTS - 1).astype(I32)
    n_valid = tile_end[-1:].astype(I32)
    pos = _slots(idx, rank, start, tm=ROUTE_TILE).reshape(t * TOP_K)
    n_slots = t * TOP_K
    pair = jnp.arange(n_slots, dtype=I32)
    n_rows = n_tiles * EXPERT_TILE
    spare = n_slots + jnp.arange(n_rows, dtype=I32) % EXPERT_TILE
    slot = spare.at[pos].set((pair % TOP_K) * t + pair // TOP_K)
    tok = jnp.where(slot < n_slots, slot % t, 0)
    ys = _experts(tile_expert, n_valid, tok, slot, h2u, wts["w_exp_gate"], wts["w_exp_up"],
                  wts["w_exp_down"], n_slots=n_slots)
    return _combine(w, h2u, x1, mods3, wts["g_post_ffn"], wts["w_sh_gate"], wts["w_sh_up"],
                    wts["w_sh_down"], ys, mod_row=mod_row(COMBINE_TILE), tm=COMBINE_TILE,
                    n_lat=n_lat)


def kernel(x_prompt, x_sample, cache_k, cache_v, c, c_ctx, w_ada, b_ada, g_pre_mix, g_post_mix, w_in, lambda_q1, lambda_k1, lambda_q2, lambda_k2, g_subln, conv_w, w_attn_out, w_conv_out, w_o, g_pre_ffn, g_post_ffn, w_router, router_bias, w_exp_gate, w_exp_up, w_exp_down, w_sh_gate, w_sh_up, w_sh_down):
    bp, sp, d = x_prompt.shape
    bs, ss, _ = x_sample.shape
    past = cache_k.shape[2]
    l = 0
    wts = {
        "g_pre_mix": g_pre_mix[l][None], "g_post_mix": g_post_mix[l][None],
        "g_pre_ffn": g_pre_ffn[l][None], "g_post_ffn": g_post_ffn[l][None],
        "g_subln": g_subln[l][None], "conv_w": conv_w[l],
        "lam_vecs": jnp.stack([lambda_q1[l], lambda_k1[l], lambda_q2[l], lambda_k2[l]]),
        "router_bias": router_bias[l][None],
        "w_in": w_in[l].astype(BF16), "w_attn_out": w_attn_out[l].astype(BF16),
        "w_conv_out": w_conv_out[l].astype(BF16), "w_o": w_o[l].astype(BF16),
        "w_router": w_router[l].astype(BF16),
        "w_exp_gate": w_exp_gate[l], "w_exp_up": w_exp_up[l], "w_exp_down": w_exp_down[l],
        "w_sh_gate": w_sh_gate[l].astype(BF16), "w_sh_up": w_sh_up[l].astype(BF16),
        "w_sh_down": w_sh_down[l].astype(BF16),
    }
    pad_rows = 8 - (bs + 1)
    c_rows = jnp.concatenate([c, c_ctx[None], jnp.zeros((pad_rows, d), F32)], axis=0)
    mods3 = _ada(c_rows, w_ada[l], b_ada[l][None]).reshape(8, 6, d)

    tm = ROW_TILE
    ctx_row = lambda tm_: (lambda i: bs)
    lat_row = lambda tm_: (lambda i: (i * tm_) // ss)
    all_row = lambda tm_: (lambda i: jnp.minimum((i * tm_) // ss, bs))
    x1p, h2p, lgp, proj_p = _mixer(x_prompt, mods3, ctx_row, wts, None, None,
                                   proj_dtype=F32, tm=tm, tq=sp)
    hk = 2 * N_HEADS * HEAD_DIM
    new_k = proj_p[:, hk:2 * hk].reshape(bp, 1, sp, 2, N_HEADS, HEAD_DIM)
    new_v = proj_p[:, 2 * hk:2 * hk + N_HEADS * V_HEAD_DIM].reshape(bp, 1, sp, N_HEADS, V_HEAD_DIM)

    ctx = (cache_k[:, l].reshape(bs, past, hk).astype(BF16),
           cache_v[:, l].reshape(bs, past, N_HEADS * V_HEAD_DIM).astype(BF16))
    x1s, h2s, lgs, _ = _mixer(x_sample, mods3, lat_row, wts, _rope_tables(ss), ctx,
                              proj_dtype=BF16, tm=tm, tq=QUERY_TILE)

    n_lat = bs * ss
    ys, yp = _moe(jnp.concatenate([x1s, x1p]), jnp.concatenate([h2s, h2p]),
                  jnp.concatenate([lgs, lgp]), mods3, all_row, wts, n_lat=n_lat)
    return (yp.reshape(bp, sp, d), ys.reshape(bs, ss, d), new_k, new_v)
```

```python
import functools
import math

import jax
import jax.numpy as jnp
from jax import lax
from jax.experimental import pallas as pl
from jax.experimental.pallas import tpu as pltpu

F32 = jnp.float32
BF16 = jnp.bfloat16
I32 = jnp.int32

EPS = 1e-6
N_HEADS = 8
HEAD_DIM = 128
V_HEAD_DIM = 2 * HEAD_DIM
D_CONV = 1024
N_EXPERTS = 64
TOP_K = 8
N_GROUPS = 8
TOPK_GROUPS = 4
ROUTED_SCALE = 2.5
ROPE_THETA = 10000.0
GRID_W = 64
LAM_INIT = 0.8 - 0.6 * math.exp(-0.3 * 0)

LANES = 128
HALO_ROWS = 16
VMEM_LIMIT_BYTES = 56 * 1024 * 1024
EXPERT_TILE = 256
KEY_CHUNK = 512
QUERY_TILE = 512
PROJ_TILE = (1024, 1024)
ROW_TILE = 512
POST_TILE = 256
DISPATCH_TILE = 256
COMBINE_TILE = 128
ROUTE_TILE = 512


def _params(*sem):
    return pltpu.CompilerParams(dimension_semantics=sem, vmem_limit_bytes=VMEM_LIMIT_BYTES)


def _rms(x):
    return x * lax.rsqrt(jnp.mean(x * x, axis=-1, keepdims=True) + EPS)


def _ada_kernel(c_ref, w_ref, b_ref, o_ref):
    c = c_ref[...]
    a = (c * jax.nn.sigmoid(c)).astype(BF16)
    o_ref[...] = jnp.dot(a, w_ref[...].astype(BF16), preferred_element_type=F32) + b_ref[...]


def _ada(c_rows, w_ada, b_ada, tn=1024):
    r, d = c_rows.shape
    n = w_ada.shape[1]
    return pl.pallas_call(
        _ada_kernel,
        out_shape=jax.ShapeDtypeStruct((r, n), F32),
        grid=(n // tn,),
        in_specs=[pl.BlockSpec((r, d), lambda j: (0, 0)),
                  pl.BlockSpec((d, tn), lambda j: (0, j)),
                  pl.BlockSpec((1, tn), lambda j: (0, j))],
        out_specs=pl.BlockSpec((r, tn), lambda j: (0, j)),
        compiler_params=_params("arbitrary"),
        name="ada",
    )(c_rows, w_ada, b_ada)


def _proj_kernel(*refs, rope, n_rope_tiles, sig_start):
    if rope:
        x_ref, mod_ref, g_ref, w_ref, cos_ref, sin_ref, o_ref, h_scr = refs
    else:
        x_ref, mod_ref, g_ref, w_ref, o_ref, h_scr = refs
    j = pl.program_id(1)

    @pl.when(j == 0)
    def _():
        shift = mod_ref[0, 0:1, :]
        scale = mod_ref[0, 1:2, :]
        h = (_rms(x_ref[...]) * g_ref[...]) * (1 + scale) + shift
        h_scr[...] = h.astype(BF16)

    acc = jnp.dot(h_scr[...], w_ref[...], preferred_element_type=F32)
    tn = acc.shape[1]

    if rope:
        @pl.when(j < n_rope_tiles)
        def _():
            cos = cos_ref[...]
            sin = sin_ref[...]
            lane = lax.broadcasted_iota(jnp.int32, cos.shape, 1)
            first = (lane % (HEAD_DIM // 2)) < (HEAD_DIM // 4)
            for s in range(tn // HEAD_DIM):
                xs = acc[:, s * HEAD_DIM:(s + 1) * HEAD_DIM]
                partner = jnp.where(first,
                                    pltpu.roll(xs, HEAD_DIM - HEAD_DIM // 4, axis=1),
                                    pltpu.roll(xs, HEAD_DIM // 4, axis=1))
                o_ref[:, s * HEAD_DIM:(s + 1) * HEAD_DIM] = (xs * cos + partner * sin).astype(o_ref.dtype)
        plain_lo = n_rope_tiles
    else:
        plain_lo = 0

    @pl.when((j >= plain_lo) & (j < sig_start))
    def _():
        o_ref[...] = acc.astype(o_ref.dtype)

    @pl.when(j >= sig_start)
    def _():
        o_ref[...] = jax.nn.sigmoid(acc).astype(o_ref.dtype)


def _proj(x, mods3, g, w, *, mod_row, rope_tabs, out_dtype, gate_col, rope_cols, tm, tn=512):
    t, d = x.shape
    n = w.shape[1]
    rope = rope_tabs is not None
    in_specs = [pl.BlockSpec((tm, d), lambda i, j: (i, 0)),
                pl.BlockSpec((1,) + mods3.shape[1:], lambda i, j: (mod_row(i), 0, 0)),
                pl.BlockSpec((1, d), lambda i, j: (0, 0)),
                pl.BlockSpec((d, tn), lambda i, j: (0, j))]
    args = [x, mods3, g, w]
    if rope:
        cos, sin = rope_tabs
        nblk = cos.shape[0] // tm
        in_specs += [pl.BlockSpec((tm, HEAD_DIM), lambda i, j: (i % nblk, 0)),
                     pl.BlockSpec((tm, HEAD_DIM), lambda i, j: (i % nblk, 0))]
        args += [cos, sin]
    kern = functools.partial(_proj_kernel, rope=rope, n_rope_tiles=rope_cols // tn,
                             sig_start=gate_col // tn)
    return pl.pallas_call(
        kern,
        out_shape=jax.ShapeDtypeStruct((t, n), out_dtype),
        grid=(t // tm, n // tn),
        in_specs=in_specs,
        out_specs=pl.BlockSpec((tm, tn), lambda i, j: (i, j)),
        scratch_shapes=[pltpu.VMEM((tm, d), BF16)],
        compiler_params=_params("parallel", "arbitrary"),
        name="proj_rope" if rope else "proj",
    )(*args)


def _attn_kernel(*refs, has_ctx):
    if has_ctx:
        (lam_ref, gs_ref, q1_ref, q2_ref, k1_ref, k2_ref, v_ref,
         k1c_ref, k2c_ref, vc_ref, o_ref, s_even, s_odd, m_even, m_odd) = refs
    else:
        (lam_ref, gs_ref, q1_ref, q2_ref, k1_ref, k2_ref, v_ref,
         o_ref, s_even, s_odd, m_even, m_odd) = refs
    t = pl.program_id(1)
    lv = lam_ref[...]
    lam = (jnp.exp(jnp.sum(lv[0:1] * lv[1:2], axis=-1, keepdims=True))
           - jnp.exp(jnp.sum(lv[2:3] * lv[3:4], axis=-1, keepdims=True)) + LAM_INIT)
    c = (HEAD_DIM ** -0.5) * math.log2(math.e)
    nt = (((1,), (1,)), ((), ()))
    tq = q1_ref.shape[1]

    chunks = []
    off = 0
    groups = [(k1_ref, k2_ref, v_ref)] + ([(k1c_ref, k2c_ref, vc_ref)] if has_ctx else [])
    for ka, kb, vv in groups:
        n = ka.shape[1]
        ch = min(KEY_CHUNK, n)
        for c0 in range(0, n, ch):
            chunks.append((ka, kb, vv, c0, ch, off + c0))
        off += n

    @pl.when(t == 0)
    def _():
        s_odd[...] = jnp.zeros_like(s_odd)
        m_odd[...] = jnp.zeros_like(m_odd)

    def step(s_fill, m_fill, s_drain, m_drain):
        qs = (q1_ref[0].astype(BF16), q2_ref[0].astype(BF16))
        mbs = (m_drain[0], m_drain[1])
        mrun = [jnp.full((tq, LANES), -jnp.inf, F32)] * 2
        lrun = [jnp.zeros((tq, LANES), F32)] * 2
        acc = [jnp.zeros((tq, V_HEAD_DIM), F32)] * 2
        for ka, kb, vv, c0, ch, col in chunks:
            for mp in range(2):
                k = (ka, kb)[mp][0, c0:c0 + ch, :].astype(BF16)
                s = lax.dot_general(qs[mp], k, nt, preferred_element_type=F32)
                s_fill[mp, :, col:col + ch] = s
                for g in range(ch // LANES):
                    mrun[mp] = jnp.maximum(mrun[mp], s[:, g * LANES:(g + 1) * LANES])
            v = vv[0, c0:c0 + ch, :].astype(BF16)
            for mp in range(2):
                ps = []
                for g in range(ch // LANES):
                    sg = s_drain[mp, :, col + g * LANES:col + (g + 1) * LANES]
                    pg = jnp.exp2((sg - mbs[mp]) * c)
                    lrun[mp] = lrun[mp] + pg
                    ps.append(pg.astype(BF16))
                acc[mp] = acc[mp] + jnp.dot(jnp.concatenate(ps, axis=1), v, preferred_element_type=F32)
        for mp in range(2):
            m_fill[mp] = jnp.broadcast_to(jnp.max(mrun[mp], axis=-1, keepdims=True), (tq, LANES))
        l1 = jnp.sum(lrun[0], axis=-1, keepdims=True)
        l2 = jnp.sum(lrun[1], axis=-1, keepdims=True)
        o = acc[0] * (1.0 / l1) - acc[1] * (lam / l2)
        o_ref[0] = ((_rms(o) * gs_ref[...]) * (1 - LAM_INIT)).astype(o_ref.dtype)

    @pl.when(t % 2 == 0)
    def _():
        step(s_even, m_even, s_odd, m_odd)

    @pl.when(t % 2 == 1)
    def _():
        step(s_odd, m_odd, s_even, m_even)


def _attention(proj3, lam_vecs, g_subln, ctx, *, tq):
    b, n, _ = proj3.shape
    h = N_HEADS
    nq = n // tq
    n_tiles = h * nq
    has_ctx = ctx is not None
    f_head = lambda t: jnp.minimum(t, n_tiles - 1) // nq
    f_row = lambda t: jnp.minimum(t, n_tiles - 1) % nq
    d_head = lambda t: jnp.maximum(t - 1, 0) // nq
    d_row = lambda t: jnp.maximum(t - 1, 0) % nq
    q_spec = lambda off: pl.BlockSpec((1, tq, HEAD_DIM), lambda bi, t: (bi, f_row(t), off + f_head(t)))
    k_spec = lambda rows, off: pl.BlockSpec((1, rows, HEAD_DIM), lambda bi, t: (bi, 0, off + f_head(t)))
    v_spec = lambda rows, off: pl.BlockSpec((1, rows, V_HEAD_DIM), lambda bi, t: (bi, 0, off + d_head(t)))
    in_specs = [pl.BlockSpec((4, HEAD_DIM), lambda bi, t: (0, 0)),
                pl.BlockSpec((1, V_HEAD_DIM), lambda bi, t: (0, 0)),
                q_spec(0), q_spec(h), k_spec(n, 2 * h), k_spec(n, 3 * h), v_spec(n, 2 * h)]
    args = [lam_vecs, g_subln, proj3, proj3, proj3, proj3, proj3]
    n_keys = n
    if has_ctx:
        ck, cv = ctx
        p = ck.shape[1]
        n_keys += p
        in_specs += [k_spec(p, 0), k_spec(p, h), v_spec(p, 0)]
        args += [ck, ck, cv]
    scores = pltpu.VMEM((2, tq, n_keys), F32)
    maxima = pltpu.VMEM((2, tq, LANES), F32)
    return pl.pallas_call(
        functools.partial(_attn_kernel, has_ctx=has_ctx),
        out_shape=jax.ShapeDtypeStruct((b, n, h * V_HEAD_DIM), BF16),
        grid=(b, n_tiles + 1),
        in_specs=in_specs,
        out_specs=pl.BlockSpec((1, tq, V_HEAD_DIM), lambda bi, t: (bi, d_row(t), d_head(t))),
        scratch_shapes=[scores, scores, maxima, maxima],
        compiler_params=_params("parallel", "arbitrary"),
        name="attn_ctx" if has_ctx else "attn",
    )(*args)


def _merge_kernel(attn_ref, cb_ref, cc_ref, cx_ref, ccp_ref, cxp_ref, ccn_ref, cxn_ref,
                  cw_ref, wa_ref, wc_ref, ga_ref, gc_ref, o_ref, conv_scr, *, seq_len):
    i = pl.program_id(0)
    j = pl.program_id(1)
    tm = cc_ref.shape[0]

    @pl.when(j == 0)
    def _():
        u = cc_ref[...].astype(F32) * cx_ref[...].astype(F32)
        u_before = (ccp_ref[...].astype(F32) * cxp_ref[...].astype(F32))[HALO_ROWS - 1:HALO_ROWS, :]
        u_after = (ccn_ref[...].astype(F32) * cxn_ref[...].astype(F32))[0:1, :]
        row = lax.broadcasted_iota(jnp.int32, (tm, 1), 0)
        pos = (i * tm + row) % seq_len
        u_prev = jnp.where(row == 0, u_before, pltpu.roll(u, 1, axis=0))
        u_prev = jnp.where(pos == 0, 0.0, u_prev)
        u_next = jnp.where(row == tm - 1, u_after, pltpu.roll(u, tm - 1, axis=0))
        u_next = jnp.where(pos == seq_len - 1, 0.0, u_next)
        cw = cw_ref[...]
        conv = u_prev * cw[0:1, :] + u * cw[1:2, :] + u_next * cw[2:3, :]
        conv_scr[...] = (cb_ref[...].astype(F32) * conv).astype(BF16)

    ya = jnp.dot(attn_ref[...], wa_ref[...], preferred_element_type=F32)
    yc = jnp.dot(conv_scr[...], wc_ref[...], preferred_element_type=F32)
    o_ref[...] = (ga_ref[...].astype(F32) * ya + gc_ref[...].astype(F32) * yc).astype(o_ref.dtype)


def _merge(attn, proj, conv_w, w_attn_out, w_conv_out, *, seq_len, cols, tm, tn=1024):
    t, aw = attn.shape
    d = w_attn_out.shape[1]
    cb0, cc0, cx0, ga0, gc0 = (cols[k] for k in ("cb", "cc", "cx", "ga", "gc"))
    hb = tm // HALO_ROWS
    last_halo = t // HALO_ROWS - 1
    cblk = lambda c0: pl.BlockSpec((tm, D_CONV), lambda i, j: (i, c0 // D_CONV))
    halo_prev = lambda c0: pl.BlockSpec(
        (HALO_ROWS, D_CONV), lambda i, j: (jnp.maximum(i * hb - 1, 0), c0 // D_CONV))
    halo_next = lambda c0: pl.BlockSpec(
        (HALO_ROWS, D_CONV), lambda i, j: (jnp.minimum((i + 1) * hb, last_halo), c0 // D_CONV))
    in_specs = [pl.BlockSpec((tm, aw), lambda i, j: (i, 0)),
                cblk(cb0), cblk(cc0), cblk(cx0),
                halo_prev(cc0), halo_prev(cx0), halo_next(cc0), halo_next(cx0),
                pl.BlockSpec(conv_w.shape, lambda i, j: (0, 0)),
                pl.BlockSpec((aw, tn), lambda i, j: (0, j)),
                pl.BlockSpec((D_CONV, tn), lambda i, j: (0, j)),
                pl.BlockSpec((tm, tn), lambda i, j: (i, ga0 // tn + j)),
                pl.BlockSpec((tm, tn), lambda i, j: (i, gc0 // tn + j))]
    return pl.pallas_call(
        functools.partial(_merge_kernel, seq_len=seq_len),
        out_shape=jax.ShapeDtypeStruct((t, d), BF16),
        grid=(t // tm, d // tn),
        in_specs=in_specs,
        out_specs=pl.BlockSpec((tm, tn), lambda i, j: (i, j)),
        scratch_shapes=[pltpu.VMEM((tm, D_CONV), BF16)],
        compiler_params=_params("parallel", "arbitrary"),
        name="merge",
    )(attn, proj, proj, proj, proj, proj, proj, proj, conv_w, w_attn_out, w_conv_out, proj, proj)


def _post_kernel(ml_ref, mc_ref, xl_ref, xc_ref, mod_ref, gpost_ref, gpre_ref, wo_ref, wr_ref,
                 x1_ref, h2_ref, lg_ref, *, lat_tiles):
    def body(m_ref, x_ref):
        y = jnp.dot(m_ref[...], wo_ref[...], preferred_element_type=F32)
        gate1 = mod_ref[0, 2:3, :]
        shift2 = mod_ref[0, 3:4, :]
        scale2 = mod_ref[0, 4:5, :]
        x1 = x_ref[...] + gate1 * (_rms(y) * gpost_ref[...])
        x1_ref[...] = x1
        h2 = (_rms(x1) * gpre_ref[...]) * (1 + scale2) + shift2
        h2_ref[...] = h2
        lg_ref[...] = jnp.dot(h2.astype(BF16), wr_ref[...], preferred_element_type=F32)

    i = pl.program_id(0)

    @pl.when(i < lat_tiles)
    def _():
        body(ml_ref, xl_ref)

    @pl.when(i >= lat_tiles)
    def _():
        body(mc_ref, xc_ref)


def _post(merged_lat, merged_ctx, x_lat, x_ctx, mods3, g_post, g_pre, w_o, w_router, *, mod_row, tm):
    n_lat, d = x_lat.shape
    t = n_lat + x_ctx.shape[0]
    e = w_router.shape[1]
    lat_tiles = n_lat // tm
    row = lambda i: (i, 0)
    lat = lambda i: (jnp.minimum(i, lat_tiles - 1), 0)
    ctx = lambda i: (jnp.maximum(i - lat_tiles, 0), 0)
    const = lambda i: (0, 0)
    return pl.pallas_call(
        functools.partial(_post_kernel, lat_tiles=lat_tiles),
        out_shape=(jax.ShapeDtypeStruct((t, d), F32),
                   jax.ShapeDtypeStruct((t, d), F32),
                   jax.ShapeDtypeStruct((t, e), F32)),
        grid=(t // tm,),
        in_specs=[pl.BlockSpec((tm, d), lat), pl.BlockSpec((tm, d), ctx),
                  pl.BlockSpec((tm, d), lat), pl.BlockSpec((tm, d), ctx),
                  pl.BlockSpec((1,) + mods3.shape[1:], lambda i: (mod_row(i), 0, 0)),
                  pl.BlockSpec((1, d), const), pl.BlockSpec((1, d), const),
                  pl.BlockSpec((d, d), const), pl.BlockSpec((d, e), const)],
        out_specs=(pl.BlockSpec((tm, d), row), pl.BlockSpec((tm, d), row),
                   pl.BlockSpec((tm, e), row)),
        compiler_params=_params("parallel"),
        name="post",
    )(merged_lat, merged_ctx, x_lat, x_ctx, mods3, g_post, g_pre, w_o, w_router)


def _route_kernel(lg_ref, bias_ref, idx_ref, w_ref, rank_ref, cnt_ref, carry_scr, below_scr):
    @pl.when(pl.program_id(0) == 0)
    def _():
        carry_scr[...] = jnp.zeros_like(carry_scr)
        r_i = lax.broadcasted_iota(I32, below_scr.shape, 0)
        c_i = lax.broadcasted_iota(I32, below_scr.shape, 1)
        below_scr[...] = jnp.where(c_i < r_i, 1.0, 0.0).astype(BF16)

    s = jax.nn.sigmoid(lg_ref[...])
    biased = s + bias_ref[...]
    tm, e = s.shape
    per = e // N_GROUPS
    lane = lax.broadcasted_iota(I32, (tm, e), 1)
    grp = lane // per
    neg = -jnp.inf

    def first_argmax(v):
        m = jnp.max(v, axis=-1, keepdims=True)
        idx = jnp.min(jnp.where(v == m, lane, e), axis=-1, keepdims=True)
        return m, idx

    scores = []
    for g in range(N_GROUPS):
        vg = jnp.where(grp == g, biased, neg)
        m1, i1 = first_argmax(vg)
        m2 = jnp.max(jnp.where(lane == i1, neg, vg), axis=-1, keepdims=True)
        scores.append(m1 + m2)
    group_score = jnp.zeros((tm, e), F32)
    for g in range(N_GROUPS):
        group_score = jnp.where(grp == g, scores[g], group_score)
    one_per_group = (lane % per) == 0
    allowed = jnp.zeros((tm, e), jnp.bool_)
    for g in range(N_GROUPS):
        wins = (group_score > scores[g]) | ((group_score == scores[g]) & (grp < g))
        beaten_by = jnp.sum(jnp.where(one_per_group & wins, 1.0, 0.0), axis=-1, keepdims=True)
        allowed = allowed | ((grp == g) & (beaten_by < TOPK_GROUPS))
    masked = jnp.where(allowed, biased, neg)
    hits, ids = [], []
    for _ in range(TOP_K):
        _, idx = first_argmax(masked)
        hit = lane == idx
        hits.append(hit)
        ids.append(idx)
        masked = jnp.where(hit, neg, masked)
    chosen = functools.reduce(jnp.logical_or, hits)
    w = jnp.where(chosen, s, 0.0)
    w = w / jnp.sum(w, axis=-1, keepdims=True) * ROUTED_SCALE

    onehot = jnp.where(chosen, 1.0, 0.0)
    rank = jnp.dot(below_scr[...], onehot.astype(BF16), preferred_element_type=F32) + carry_scr[...]
    carry = carry_scr[...] + jnp.sum(onehot, axis=0, keepdims=True)
    carry_scr[...] = carry
    cnt_ref[...] = carry.astype(I32)

    col = lax.broadcasted_iota(I32, (tm, TOP_K), 1)
    idx_o = jnp.zeros((tm, TOP_K), I32)
    w_o = jnp.zeros((tm, TOP_K), F32)
    rank_o = jnp.zeros((tm, TOP_K), F32)
    for k in range(TOP_K):
        wk = jnp.sum(jnp.where(hits[k], w, 0.0), axis=-1, keepdims=True)
        rk = jnp.sum(jnp.where(hits[k], rank, 0.0), axis=-1, keepdims=True)
        idx_o = jnp.where(col == k, ids[k], idx_o)
        w_o = jnp.where(col == k, wk, w_o)
        rank_o = jnp.where(col == k, rk, rank_o)
    idx_ref[...] = idx_o
    w_ref[...] = w_o
    rank_ref[...] = rank_o.astype(I32)


def _route(logits, bias, *, tm):
    t, e = logits.shape
    row = lambda i: (i, 0)
    return pl.pallas_call(
        _route_kernel,
        out_shape=(jax.ShapeDtypeStruct((t, TOP_K), I32), jax.ShapeDtypeStruct((t, TOP_K), F32),
                   jax.ShapeDtypeStruct((t, TOP_K), I32), jax.ShapeDtypeStruct((1, e), I32)),
        grid=(t // tm,),
        in_specs=[pl.BlockSpec((tm, e), row), pl.BlockSpec((1, e), lambda i: (0, 0))],
        out_specs=(pl.BlockSpec((tm, TOP_K), row), pl.BlockSpec((tm, TOP_K), row),
                   pl.BlockSpec((tm, TOP_K), row), pl.BlockSpec((1, e), lambda i: (0, 0))),
        scratch_shapes=[pltpu.VMEM((1, e), F32), pltpu.VMEM((tm, tm), BF16)],
        compiler_params=_params("arbitrary"),
        name="route",
    )(logits, bias)


def _slots_kernel(idx_ref, rank_ref, start_ref, o_ref):
    idx = idx_ref[...]
    tm = idx.shape[0]
    e = start_ref.shape[1]
    lane = lax.broadcasted_iota(I32, (tm, e), 1)
    col = lax.broadcasted_iota(I32, (tm, TOP_K), 1)
    base = jnp.zeros((tm, TOP_K), I32)
    for k in range(TOP_K):
        sk = jnp.sum(jnp.where(lane == idx[:, k:k + 1], start_ref[...], 0), axis=-1, keepdims=True)
        base = jnp.where(col == k, sk, base)
    o_ref[...] = base + rank_ref[...]


def _slots(idx, rank, start, *, tm):
    t = idx.shape[0]
    row = lambda i: (i, 0)
    return pl.pallas_call(
        _slots_kernel,
        out_shape=jax.ShapeDtypeStruct((t, TOP_K), I32),
        grid=(t // tm,),
        in_specs=[pl.BlockSpec((tm, TOP_K), row), pl.BlockSpec((tm, TOP_K), row),
                  pl.BlockSpec(start.shape, lambda i: (0, 0))],
        out_specs=pl.BlockSpec((tm, TOP_K), row),
        compiler_params=_params("parallel"),
        name="slots",
    )(idx, rank, start)


def _row_copy(src_ref, src_row, dst_ref, dst_row, sem):
    return pltpu.make_async_copy(src_ref.at[pl.ds(src_row, 1)], dst_ref.at[pl.ds(dst_row, 1)], sem)


def _dispatch_kernel(last_ref, used_ref, nv_ref, pos_ref, x_ref, xs_ref, zbuf, sem, zsem):
    tm = x_ref.shape[0]
    n_tiles = xs_ref.shape[0] // EXPERT_TILE

    def zero_tile(row):
        return pltpu.make_async_copy(zbuf, xs_ref.at[pl.ds(row, EXPERT_TILE)], zsem)

    def for_each_partly_empty_tile(fn):
        def per_expert(e, carry):
            @pl.when(used_ref[e] > 0)
            def _():
                fn(pl.multiple_of(last_ref[e], EXPERT_TILE))
            return carry

        def per_tile(j, carry):
            fn(pl.multiple_of(j * EXPERT_TILE, EXPERT_TILE))
            return carry

        lax.fori_loop(0, N_EXPERTS, per_expert, 0)
        lax.fori_loop(nv_ref[0], n_tiles, per_tile, 0)

    @pl.when(pl.program_id(0) == 0)
    def _():
        zbuf[...] = jnp.zeros_like(zbuf)
        for_each_partly_empty_tile(lambda row: zero_tile(row).start())
        for_each_partly_empty_tile(lambda row: zero_tile(row).wait())

    def body(r, carry):
        for k in range(TOP_K):
            _row_copy(x_ref, r, xs_ref, pos_ref[r * TOP_K + k], sem).start()
        return carry

    lax.fori_loop(0, tm, body, 0)
    n = tm * TOP_K
    pltpu.make_async_copy(xs_ref.at[pl.ds(0, n)], xs_ref.at[pl.ds(0, n)], sem).wait()


def _dispatch(last_row, used, n_valid, pos_flat, h2u, n_rows, *, tm):
    t, dw = h2u.shape
    return pl.pallas_call(
        _dispatch_kernel,
        out_shape=jax.ShapeDtypeStruct((n_rows, dw), F32),
        grid_spec=pltpu.PrefetchScalarGridSpec(
            num_scalar_prefetch=3,
            grid=(t // tm,),
            in_specs=[pl.BlockSpec((tm * TOP_K,), lambda i, *_: (i,), memory_space=pltpu.SMEM),
                      pl.BlockSpec((tm, dw), lambda i, *_: (i, 0))],
            out_specs=pl.BlockSpec(memory_space=pl.ANY),
            scratch_shapes=[pltpu.VMEM((EXPERT_TILE, dw), F32),
                            pltpu.SemaphoreType.DMA, pltpu.SemaphoreType.DMA]),
        compiler_params=_params("arbitrary"),
        name="dispatch",
    )(last_row, used, n_valid, pos_flat, h2u)


def _ffn(x, wg, wu, wd):
    a = jnp.dot(x, wg, preferred_element_type=F32)
    b = jnp.dot(x, wu, preferred_element_type=F32)
    hid = (a * jax.nn.sigmoid(a)) * b
    return jnp.dot(hid.astype(BF16), wd, preferred_element_type=F32)


def _expert_kernel(te_ref, nv_ref, x_ref, wg_ref, wu_ref, wd_ref, o_ref, wg_b, wu_b, wd_b):
    j = pl.program_id(0)
    live = j < nv_ref[0]
    new_expert = (j == 0) | (te_ref[j] != te_ref[jnp.maximum(j - 1, 0)])

    @pl.when(live & new_expert)
    def _():
        wg_b[...] = wg_ref[0].astype(BF16)
        wu_b[...] = wu_ref[0].astype(BF16)
        wd_b[...] = wd_ref[0].astype(BF16)

    @pl.when(live)
    def _():
        o_ref[...] = _ffn(x_ref[...].astype(BF16), wg_b[...], wu_b[...], wd_b[...])

    @pl.when(jnp.logical_not(live))
    def _():
        o_ref[...] = jnp.zeros_like(o_ref)


def _experts(tile_expert, n_valid, xs, wg, wu, wd):
    n_rows, dw = xs.shape
    ne, d, de = wg.shape
    tm = EXPERT_TILE
    row = lambda j, te, nv: (jnp.minimum(j, nv[0] - 1), 0)
    wsel = lambda j, te, nv: (te[j], 0, 0)
    return pl.pallas_call(
        _expert_kernel,
        out_shape=jax.ShapeDtypeStruct((n_rows, dw), F32),
        grid_spec=pltpu.PrefetchScalarGridSpec(
            num_scalar_prefetch=2,
            grid=(n_rows // tm,),
            in_specs=[pl.BlockSpec((tm, dw), row),
                      pl.BlockSpec((1, d, de), wsel), pl.BlockSpec((1, d, de), wsel),
                      pl.BlockSpec((1, de, d), wsel)],
            out_specs=pl.BlockSpec((tm, dw), lambda j, te, nv: (j, 0)),
            scratch_shapes=[pltpu.VMEM((d, de), BF16), pltpu.VMEM((d, de), BF16),
                            pltpu.VMEM((de, d), BF16)]),
        compiler_params=_params("arbitrary"),
        name="experts",
    )(tile_expert, n_valid, xs, wg, wu, wd)


def _combine_kernel(pos_ref, posn_ref, w_ref, h_ref, x1_ref, mod_ref, gpost_ref,
                    sg_ref, su_ref, sd_ref, ys_ref, olat_ref, octx_ref,
                    buf_even, buf_odd, sem_even, sem_odd, *, lat_tiles):
    tm = h_ref.shape[0]
    n = tm * TOP_K
    i = pl.program_id(0)
    last = pl.num_programs(0) - 1

    def drain(buf, sem):
        pltpu.make_async_copy(ys_ref.at[pl.ds(0, n)], buf, sem).wait()

    @pl.when(i == 0)
    def _():
        def body(r, carry):
            for k in range(TOP_K):
                _row_copy(ys_ref, pos_ref[r * TOP_K + k], buf_even, k * tm + r, sem_even).start()
            return carry
        lax.fori_loop(0, tm, body, 0)

    def step(buf, sem, buf_next, sem_next):
        drain(buf, sem)
        for r in range(tm):
            for k in range(TOP_K):
                _row_copy(ys_ref, posn_ref[r * TOP_K + k], buf_next, k * tm + r, sem_next).start()
        acc = _ffn(h_ref[...].astype(BF16), sg_ref[...], su_ref[...], sd_ref[...])
        w = w_ref[...]
        for k in range(TOP_K):
            acc = acc + w[:, k:k + 1] * buf[k * tm:(k + 1) * tm, :]
        gate2 = mod_ref[0, 5:6, :]
        res = x1_ref[...] + gate2 * (_rms(acc) * gpost_ref[...])

        @pl.when(i < lat_tiles)
        def _():
            olat_ref[...] = res

        @pl.when(i >= lat_tiles)
        def _():
            octx_ref[...] = res

        @pl.when(i == last)
        def _():
            drain(buf_next, sem_next)

    @pl.when(i % 2 == 0)
    def _():
        step(buf_even, sem_even, buf_odd, sem_odd)

    @pl.when(i % 2 == 1)
    def _():
        step(buf_odd, sem_odd, buf_even, sem_even)


def _combine(pos_flat, w, h2u, x1, mods3, g_post, sg, su, sd, ys, *, mod_row, tm, n_lat):
    t, d = x1.shape
    dw = h2u.shape[1]
    ds = sg.shape[1]
    steps = t // tm
    lat_tiles = n_lat // tm
    row = lambda i: (i, 0)
    const = lambda i: (0, 0)
    buf = pltpu.VMEM((tm * TOP_K, dw), F32)
    return pl.pallas_call(
        functools.partial(_combine_kernel, lat_tiles=lat_tiles),
        out_shape=(jax.ShapeDtypeStruct((n_lat, d), F32), jax.ShapeDtypeStruct((t - n_lat, d), F32)),
        grid=(steps,),
        in_specs=[pl.BlockSpec((tm * TOP_K,), lambda i: (i,), memory_space=pltpu.SMEM),
                  pl.BlockSpec((tm * TOP_K,), lambda i: (jnp.minimum(i + 1, steps - 1),),
                               memory_space=pltpu.SMEM),
                  pl.BlockSpec((tm, TOP_K), row), pl.BlockSpec((tm, dw), row),
                  pl.BlockSpec((tm, d), row),
                  pl.BlockSpec((1,) + mods3.shape[1:], lambda i: (mod_row(i), 0, 0)),
                  pl.BlockSpec((1, d), const),
                  pl.BlockSpec((d, ds), const), pl.BlockSpec((d, ds), const),
                  pl.BlockSpec((ds, d), const),
                  pl.BlockSpec(memory_space=pl.ANY)],
        out_specs=(pl.BlockSpec((tm, d), lambda i: (jnp.minimum(i, lat_tiles - 1), 0)),
                   pl.BlockSpec((tm, d), lambda i: (jnp.maximum(i - lat_tiles, 0), 0))),
        scratch_shapes=[buf, buf, pltpu.SemaphoreType.DMA, pltpu.SemaphoreType.DMA],
        compiler_params=_params("arbitrary"),
        name="combine",
    )(pos_flat, pos_flat, w, h2u, x1, mods3, g_post, sg, su, sd, ys)


def _rope_tables(n_tokens):
    rows = n_tokens // GRID_W
    row = jnp.repeat(jnp.arange(rows, dtype=F32), GRID_W)
    col = jnp.tile(jnp.arange(GRID_W, dtype=F32), rows)
    n_freq = HEAD_DIM // 4
    freqs = ROPE_THETA ** (-jnp.arange(n_freq, dtype=F32) / n_freq)
    ang_r = row[:, None] * freqs
    ang_c = col[:, None] * freqs
    ang = jnp.concatenate([ang_r, ang_r, ang_c, ang_c], axis=-1)
    sign = jnp.where((jnp.arange(HEAD_DIM) % (HEAD_DIM // 2)) < HEAD_DIM // 4, -1.0, 1.0)
    return jnp.cos(ang), jnp.sin(ang) * sign


def _mixer(x3, mods3, mod_row, wts, rope_tabs, ctx, *, proj_dtype, tm, tq):
    b, n, d = x3.shape
    x = x3.reshape(b * n, d)
    qk_cols = 4 * N_HEADS * HEAD_DIM
    cols = {"cb": qk_cols + N_HEADS * V_HEAD_DIM}
    cols["cc"] = cols["cb"] + D_CONV
    cols["cx"] = cols["cc"] + D_CONV
    cols["ga"] = cols["cx"] + D_CONV
    cols["gc"] = cols["ga"] + d
    proj = _proj(x, mods3, wts["g_pre_mix"], wts["w_in"], mod_row=mod_row(PROJ_TILE[0]),
                 rope_tabs=rope_tabs, out_dtype=proj_dtype, gate_col=cols["ga"], rope_cols=qk_cols,
                 tm=PROJ_TILE[0], tn=PROJ_TILE[1])
    attn = _attention(proj.reshape(b, n, -1), wts["lam_vecs"], wts["g_subln"], ctx, tq=tq)
    merged = _merge(attn.reshape(b * n, -1), proj, wts["conv_w"], wts["w_attn_out"],
                    wts["w_conv_out"], seq_len=n, cols=cols, tm=tm)
    return merged, x, proj


def _moe(x1, h2u, logits, mods3, mod_row, wts, *, n_lat):
    t = x1.shape[0]
    idx, w, rank, counts = _route(logits, wts["router_bias"], tm=ROUTE_TILE)
    tiles_per = (counts[0] + EXPERT_TILE - 1) // EXPERT_TILE
    tile_end = jnp.cumsum(tiles_per)
    start = ((tile_end - tiles_per) * EXPERT_TILE)[None].astype(I32)
    n_tiles = t * TOP_K // EXPERT_TILE + N_EXPERTS
    tile_expert = jnp.minimum(
        jnp.sum(jnp.arange(n_tiles, dtype=I32)[:, None] >= tile_end[None, :], axis=1),
        N_EXPERTS - 1).astype(I32)
    n_valid = tile_end[-1:].astype(I32)
    last_row = ((tile_end - 1) * EXPERT_TILE).astype(I32)
    used = (tiles_per > 0).astype(I32)
    pos = _slots(idx, rank, start, tm=ROUTE_TILE).reshape(t * TOP_K)
    xs = _dispatch(last_row, used, n_valid, pos, h2u, n_tiles * EXPERT_TILE, tm=DISPATCH_TILE)
    ys = _experts(tile_expert, n_valid, xs, wts["w_exp_gate"], wts["w_exp_up"], wts["w_exp_down"])
    return _combine(pos, w, h2u, x1, mods3, wts["g_post_ffn"], wts["w_sh_gate"], wts["w_sh_up"],
                    wts["w_sh_down"], ys, mod_row=mod_row(COMBINE_TILE), tm=COMBINE_TILE,
                    n_lat=n_lat)


def kernel(x_prompt, x_sample, cache_k, cache_v, c, c_ctx, w_ada, b_ada, g_pre_mix, g_post_mix, w_in, lambda_q1, lambda_k1, lambda_q2, lambda_k2, g_subln, conv_w, w_attn_out, w_conv_out, w_o, g_pre_ffn, g_post_ffn, w_router, router_bias, w_exp_gate, w_exp_up, w_exp_down, w_sh_gate, w_sh_up, w_sh_down):
    bp, sp, d = x_prompt.shape
    bs, ss, _ = x_sample.shape
    past = cache_k.shape[2]
    l = 0
    wts = {
        "g_pre_mix": g_pre_mix[l][None], "g_post_mix": g_post_mix[l][None],
        "g_pre_ffn": g_pre_ffn[l][None], "g_post_ffn": g_post_ffn[l][None],
        "g_subln": g_subln[l][None], "conv_w": conv_w[l],
        "lam_vecs": jnp.stack([lambda_q1[l], lambda_k1[l], lambda_q2[l], lambda_k2[l]]),
        "router_bias": router_bias[l][None],
        "w_in": w_in[l].astype(BF16), "w_attn_out": w_attn_out[l].astype(BF16),
        "w_conv_out": w_conv_out[l].astype(BF16), "w_o": w_o[l].astype(BF16),
        "w_router": w_router[l].astype(BF16),
        "w_exp_gate": w_exp_gate[l], "w_exp_up": w_exp_up[l], "w_exp_down": w_exp_down[l],
        "w_sh_gate": w_sh_gate[l].astype(BF16), "w_sh_up": w_sh_up[l].astype(BF16),
        "w_sh_down": w_sh_down[l].astype(BF16),
    }
    pad_rows = 8 - (bs + 1)
    c_rows = jnp.concatenate([c, c_ctx[None], jnp.zeros((pad_rows, d), F32)], axis=0)
    mods3 = _ada(c_rows, w_ada[l], b_ada[l][None]).reshape(8, 6, d)

    tm = ROW_TILE
    ctx_row = lambda tm_: (lambda i: bs)
    lat_row = lambda tm_: (lambda i: (i * tm_) // ss)
    all_row = lambda tm_: (lambda i: jnp.minimum((i * tm_) // ss, bs))
    merged_p, xp, proj_p = _mixer(x_prompt, mods3, ctx_row, wts, None, None,
                                  proj_dtype=F32, tm=tm, tq=sp)
    hk = 2 * N_HEADS * HEAD_DIM
    new_k = proj_p[:, hk:2 * hk].reshape(bp, 1, sp, 2, N_HEADS, HEAD_DIM)
    new_v = proj_p[:, 2 * hk:2 * hk + N_HEADS * V_HEAD_DIM].reshape(bp, 1, sp, N_HEADS, V_HEAD_DIM)

    ctx = (cache_k[:, l].reshape(bs, past, hk).astype(BF16),
           cache_v[:, l].reshape(bs, past, N_HEADS * V_HEAD_DIM).astype(BF16))
    merged_s, xs, _ = _mixer(x_sample, mods3, lat_row, wts, _rope_tables(ss), ctx,
                             proj_dtype=BF16, tm=tm, tq=QUERY_TILE)

    n_lat = bs * ss
    x1, h2, logits = _post(merged_s, merged_p, xs, xp, mods3, wts["g_post_mix"], wts["g_pre_ffn"],
                           wts["w_o"], wts["w_router"], mod_row=all_row(POST_TILE), tm=POST_TILE)
    ys, yp = _moe(x1, h2, logits, mods3, all_row, wts, n_lat=n_lat)
    return (yp.reshape(bp, sp, d), ys.reshape(bs, ss, d), new_k, new_v)
```

```python
import functools
import math

import jax
import jax.numpy as jnp
from jax import lax
from jax.experimental import pallas as pl
from jax.experimental.pallas import tpu as pltpu

F32 = jnp.float32
BF16 = jnp.bfloat16
I32 = jnp.int32

EPS = 1e-6
N_HEADS = 8
HEAD_DIM = 128
V_HEAD_DIM = 2 * HEAD_DIM
D_CONV = 1024
N_EXPERTS = 64
TOP_K = 8
N_GROUPS = 8
TOPK_GROUPS = 4
ROUTED_SCALE = 2.5
ROPE_THETA = 10000.0
GRID_W = 64
LAM_INIT = 0.8 - 0.6 * math.exp(-0.3 * 0)

LANES = 128
HALO_ROWS = 16
VMEM_LIMIT_BYTES = 56 * 1024 * 1024
EXPERT_TILE = 256
KEY_CHUNK = 512
PROJ_SUB = 256
QUERY_TILE = 512
PROJ_TILE = (1024, 1024)
ROW_TILE = 512
POST_TILE = 256
DISPATCH_TILE = 256
COMBINE_TILE = 128
ROUTE_TILE = 512


def _params(*sem):
    return pltpu.CompilerParams(dimension_semantics=sem, vmem_limit_bytes=VMEM_LIMIT_BYTES)


def _rms(x):
    return x * lax.rsqrt(jnp.mean(x * x, axis=-1, keepdims=True) + EPS)


def _ada_kernel(c_ref, w_ref, b_ref, o_ref):
    c = c_ref[...]
    a = (c * jax.nn.sigmoid(c)).astype(BF16)
    o_ref[...] = jnp.dot(a, w_ref[...].astype(BF16), preferred_element_type=F32) + b_ref[...]


def _ada(c_rows, w_ada, b_ada, tn=1024):
    r, d = c_rows.shape
    n = w_ada.shape[1]
    return pl.pallas_call(
        _ada_kernel,
        out_shape=jax.ShapeDtypeStruct((r, n), F32),
        grid=(n // tn,),
        in_specs=[pl.BlockSpec((r, d), lambda j: (0, 0)),
                  pl.BlockSpec((d, tn), lambda j: (0, j)),
                  pl.BlockSpec((1, tn), lambda j: (0, j))],
        out_specs=pl.BlockSpec((r, tn), lambda j: (0, j)),
        compiler_params=_params("arbitrary"),
        name="ada",
    )(c_rows, w_ada, b_ada)


def _proj_kernel(*refs, rope, n_rope_tiles, sig_start):
    if rope:
        x_ref, mod_ref, g_ref, w_ref, cos_ref, sin_ref, o_ref, h_scr = refs
    else:
        x_ref, mod_ref, g_ref, w_ref, o_ref, h_scr = refs
    j = pl.program_id(1)

    @pl.when(j == 0)
    def _():
        shift = mod_ref[0, 0:1, :]
        scale = mod_ref[0, 1:2, :]
        h = (_rms(x_ref[...]) * g_ref[...]) * (1 + scale) + shift
        h_scr[...] = h.astype(BF16)

    tn = w_ref.shape[1]

    def pieces():
        for lo in range(0, tn, PROJ_SUB):
            yield lo, jnp.dot(h_scr[...], w_ref[:, lo:lo + PROJ_SUB], preferred_element_type=F32)

    if rope:
        @pl.when(j < n_rope_tiles)
        def _():
            cos = cos_ref[...]
            sin = sin_ref[...]
            lane = lax.broadcasted_iota(jnp.int32, cos.shape, 1)
            first = (lane % (HEAD_DIM // 2)) < (HEAD_DIM // 4)
            for lo, acc in pieces():
                for s in range(PROJ_SUB // HEAD_DIM):
                    xs = acc[:, s * HEAD_DIM:(s + 1) * HEAD_DIM]
                    partner = jnp.where(first,
                                        pltpu.roll(xs, HEAD_DIM - HEAD_DIM // 4, axis=1),
                                        pltpu.roll(xs, HEAD_DIM // 4, axis=1))
                    c0 = lo + s * HEAD_DIM
                    o_ref[:, c0:c0 + HEAD_DIM] = (xs * cos + partner * sin).astype(o_ref.dtype)
        plain_lo = n_rope_tiles
    else:
        plain_lo = 0

    @pl.when((j >= plain_lo) & (j < sig_start))
    def _():
        for lo, acc in pieces():
            o_ref[:, lo:lo + PROJ_SUB] = acc.astype(o_ref.dtype)

    @pl.when(j >= sig_start)
    def _():
        for lo, acc in pieces():
            o_ref[:, lo:lo + PROJ_SUB] = jax.nn.sigmoid(acc).astype(o_ref.dtype)


def _proj(x, mods3, g, w, *, mod_row, rope_tabs, out_dtype, gate_col, rope_cols, tm, tn=512):
    t, d = x.shape
    n = w.shape[1]
    rope = rope_tabs is not None
    in_specs = [pl.BlockSpec((tm, d), lambda i, j: (i, 0)),
                pl.BlockSpec((1,) + mods3.shape[1:], lambda i, j: (mod_row(i), 0, 0)),
                pl.BlockSpec((1, d), lambda i, j: (0, 0)),
                pl.BlockSpec((d, tn), lambda i, j: (0, j))]
    args = [x, mods3, g, w]
    if rope:
        cos, sin = rope_tabs
        nblk = cos.shape[0] // tm
        in_specs += [pl.BlockSpec((tm, HEAD_DIM), lambda i, j: (i % nblk, 0)),
                     pl.BlockSpec((tm, HEAD_DIM), lambda i, j: (i % nblk, 0))]
        args += [cos, sin]
    kern = functools.partial(_proj_kernel, rope=rope, n_rope_tiles=rope_cols // tn,
                             sig_start=gate_col // tn)
    return pl.pallas_call(
        kern,
        out_shape=jax.ShapeDtypeStruct((t, n), out_dtype),
        grid=(t // tm, n // tn),
        in_specs=in_specs,
        out_specs=pl.BlockSpec((tm, tn), lambda i, j: (i, j)),
        scratch_shapes=[pltpu.VMEM((tm, d), BF16)],
        compiler_params=_params("parallel", "arbitrary"),
        name="proj_rope" if rope else "proj",
    )(*args)


def _attn_kernel(*refs, has_ctx):
    if has_ctx:
        (lam_ref, gs_ref, q1_ref, q2_ref, k1_ref, k2_ref, v_ref,
         k1c_ref, k2c_ref, vc_ref, o_ref, s_even, s_odd, m_even, m_odd) = refs
    else:
        (lam_ref, gs_ref, q1_ref, q2_ref, k1_ref, k2_ref, v_ref,
         o_ref, s_even, s_odd, m_even, m_odd) = refs
    t = pl.program_id(1)
    lv = lam_ref[...]
    lam = (jnp.exp(jnp.sum(lv[0:1] * lv[1:2], axis=-1, keepdims=True))
           - jnp.exp(jnp.sum(lv[2:3] * lv[3:4], axis=-1, keepdims=True)) + LAM_INIT)
    c = (HEAD_DIM ** -0.5) * math.log2(math.e)
    nt = (((1,), (1,)), ((), ()))
    tq = q1_ref.shape[1]

    chunks = []
    off = 0
    groups = [(k1_ref, k2_ref, v_ref)] + ([(k1c_ref, k2c_ref, vc_ref)] if has_ctx else [])
    for ka, kb, vv in groups:
        n = ka.shape[1]
        ch = min(KEY_CHUNK, n)
        for c0 in range(0, n, ch):
            chunks.append((ka, kb, vv, c0, ch, off + c0))
        off += n

    @pl.when(t == 0)
    def _():
        s_odd[...] = jnp.zeros_like(s_odd)
        m_odd[...] = jnp.zeros_like(m_odd)

    def step(s_fill, m_fill, s_drain, m_drain):
        qs = (q1_ref[0].astype(BF16), q2_ref[0].astype(BF16))
        mbs = (m_drain[0], m_drain[1])
        mrun = [jnp.full((tq, LANES), -jnp.inf, F32)] * 2
        lrun = [jnp.zeros((tq, LANES), F32)] * 2
        acc = [jnp.zeros((tq, V_HEAD_DIM), F32)] * 2
        for ka, kb, vv, c0, ch, col in chunks:
            for mp in range(2):
                k = (ka, kb)[mp][0, c0:c0 + ch, :].astype(BF16)
                s = lax.dot_general(qs[mp], k, nt, preferred_element_type=F32)
                s_fill[mp, :, col:col + ch] = s
                for g in range(ch // LANES):
                    mrun[mp] = jnp.maximum(mrun[mp], s[:, g * LANES:(g + 1) * LANES])
            v = vv[0, c0:c0 + ch, :].astype(BF16)
            for mp in range(2):
                ps = []
                for g in range(ch // LANES):
                    sg = s_drain[mp, :, col + g * LANES:col + (g + 1) * LANES]
                    pg = jnp.exp2((sg - mbs[mp]) * c)
                    lrun[mp] = lrun[mp] + pg
                    ps.append(pg.astype(BF16))
                acc[mp] = acc[mp] + jnp.dot(jnp.concatenate(ps, axis=1), v, preferred_element_type=F32)
        for mp in range(2):
            m_fill[mp] = jnp.broadcast_to(jnp.max(mrun[mp], axis=-1, keepdims=True), (tq, LANES))
        l1 = jnp.sum(lrun[0], axis=-1, keepdims=True)
        l2 = jnp.sum(lrun[1], axis=-1, keepdims=True)
        o = acc[0] * (1.0 / l1) - acc[1] * (lam / l2)
        o_ref[0] = ((_rms(o) * gs_ref[...]) * (1 - LAM_INIT)).astype(o_ref.dtype)

    @pl.when(t % 2 == 0)
    def _():
        step(s_even, m_even, s_odd, m_odd)

    @pl.when(t % 2 == 1)
    def _():
        step(s_odd, m_odd, s_even, m_even)


def _attention(proj3, lam_vecs, g_subln, ctx, *, tq):
    b, n, _ = proj3.shape
    h = N_HEADS
    nq = n // tq
    n_tiles = h * nq
    has_ctx = ctx is not None
    f_head = lambda t: jnp.minimum(t, n_tiles - 1) // nq
    f_row = lambda t: jnp.minimum(t, n_tiles - 1) % nq
    d_head = lambda t: jnp.maximum(t - 1, 0) // nq
    d_row = lambda t: jnp.maximum(t - 1, 0) % nq
    q_spec = lambda off: pl.BlockSpec((1, tq, HEAD_DIM), lambda bi, t: (bi, f_row(t), off + f_head(t)))
    k_spec = lambda rows, off: pl.BlockSpec((1, rows, HEAD_DIM), lambda bi, t: (bi, 0, off + f_head(t)))
    v_spec = lambda rows, off: pl.BlockSpec((1, rows, V_HEAD_DIM), lambda bi, t: (bi, 0, off + d_head(t)))
    in_specs = [pl.BlockSpec((4, HEAD_DIM), lambda bi, t: (0, 0)),
                pl.BlockSpec((1, V_HEAD_DIM), lambda bi, t: (0, 0)),
                q_spec(0), q_spec(h), k_spec(n, 2 * h), k_spec(n, 3 * h), v_spec(n, 2 * h)]
    args = [lam_vecs, g_subln, proj3, proj3, proj3, proj3, proj3]
    n_keys = n
    if has_ctx:
        ck, cv = ctx
        p = ck.shape[1]
        n_keys += p
        in_specs += [k_spec(p, 0), k_spec(p, h), v_spec(p, 0)]
        args += [ck, ck, cv]
    scores = pltpu.VMEM((2, tq, n_keys), F32)
    maxima = pltpu.VMEM((2, tq, LANES), F32)
    return pl.pallas_call(
        functools.partial(_attn_kernel, has_ctx=has_ctx),
        out_shape=jax.ShapeDtypeStruct((b, n, h * V_HEAD_DIM), BF16),
        grid=(b, n_tiles + 1),
        in_specs=in_specs,
        out_specs=pl.BlockSpec((1, tq, V_HEAD_DIM), lambda bi, t: (bi, d_row(t), d_head(t))),
        scratch_shapes=[scores, scores, maxima, maxima],
        compiler_params=_params("parallel", "arbitrary"),
        name="attn_ctx" if has_ctx else "attn",
    )(*args)


def _merge_kernel(attn_ref, cb_ref, cc_ref, cx_ref, ccp_ref, cxp_ref, ccn_ref, cxn_ref,
                  cw_ref, wa_ref, wc_ref, ga_ref, gc_ref, o_ref, conv_scr, *, seq_len):
    i = pl.program_id(0)
    j = pl.program_id(1)
    tm = cc_ref.shape[0]

    @pl.when(j == 0)
    def _():
        u = cc_ref[...].astype(F32) * cx_ref[...].astype(F32)
        u_before = (ccp_ref[...].astype(F32) * cxp_ref[...].astype(F32))[HALO_ROWS - 1:HALO_ROWS, :]
        u_after = (ccn_ref[...].astype(F32) * cxn_ref[...].astype(F32))[0:1, :]
        row = lax.broadcasted_iota(jnp.int32, (tm, 1), 0)
        pos = (i * tm + row) % seq_len
        u_prev = jnp.where(row == 0, u_before, pltpu.roll(u, 1, axis=0))
        u_prev = jnp.where(pos == 0, 0.0, u_prev)
        u_next = jnp.where(row == tm - 1, u_after, pltpu.roll(u, tm - 1, axis=0))
        u_next = jnp.where(pos == seq_len - 1, 0.0, u_next)
        cw = cw_ref[...]
        conv = u_prev * cw[0:1, :] + u * cw[1:2, :] + u_next * cw[2:3, :]
        conv_scr[...] = (cb_ref[...].astype(F32) * conv).astype(BF16)

    ya = jnp.dot(attn_ref[...], wa_ref[...], preferred_element_type=F32)
    yc = jnp.dot(conv_scr[...], wc_ref[...], preferred_element_type=F32)
    o_ref[...] = (ga_ref[...].astype(F32) * ya + gc_ref[...].astype(F32) * yc).astype(o_ref.dtype)


def _merge(attn, proj, conv_w, w_attn_out, w_conv_out, *, seq_len, cols, tm, tn=1024):
    t, aw = attn.shape
    d = w_attn_out.shape[1]
    cb0, cc0, cx0, ga0, gc0 = (cols[k] for k in ("cb", "cc", "cx", "ga", "gc"))
    hb = tm // HALO_ROWS
    last_halo = t // HALO_ROWS - 1
    cblk = lambda c0: pl.BlockSpec((tm, D_CONV), lambda i, j: (i, c0 // D_CONV))
    halo_prev = lambda c0: pl.BlockSpec(
        (HALO_ROWS, D_CONV), lambda i, j: (jnp.maximum(i * hb - 1, 0), c0 // D_CONV))
    halo_next = lambda c0: pl.BlockSpec(
        (HALO_ROWS, D_CONV), lambda i, j: (jnp.minimum((i + 1) * hb, last_halo), c0 // D_CONV))
    in_specs = [pl.BlockSpec((tm, aw), lambda i, j: (i, 0)),
                cblk(cb0), cblk(cc0), cblk(cx0),
                halo_prev(cc0), halo_prev(cx0), halo_next(cc0), halo_next(cx0),
                pl.BlockSpec(conv_w.shape, lambda i, j: (0, 0)),
                pl.BlockSpec((aw, tn), lambda i, j: (0, j)),
                pl.BlockSpec((D_CONV, tn), lambda i, j: (0, j)),
                pl.BlockSpec((tm, tn), lambda i, j: (i, ga0 // tn + j)),
                pl.BlockSpec((tm, tn), lambda i, j: (i, gc0 // tn + j))]
    return pl.pallas_call(
        functools.partial(_merge_kernel, seq_len=seq_len),
        out_shape=jax.ShapeDtypeStruct((t, d), BF16),
        grid=(t // tm, d // tn),
        in_specs=in_specs,
        out_specs=pl.BlockSpec((tm, tn), lambda i, j: (i, j)),
        scratch_shapes=[pltpu.VMEM((tm, D_CONV), BF16)],
        compiler_params=_params("parallel", "arbitrary"),
        name="merge",
    )(attn, proj, proj, proj, proj, proj, proj, proj, conv_w, w_attn_out, w_conv_out, proj, proj)


def _post_kernel(ml_ref, mc_ref, xl_ref, xc_ref, mod_ref, gpost_ref, gpre_ref, wo_ref, wr_ref,
                 x1_ref, h2_ref, lg_ref, *, lat_tiles):
    def body(m_ref, x_ref):
        y = jnp.dot(m_ref[...], wo_ref[...], preferred_element_type=F32)
        gate1 = mod_ref[0, 2:3, :]
        shift2 = mod_ref[0, 3:4, :]
        scale2 = mod_ref[0, 4:5, :]
        x1 = x_ref[...] + gate1 * (_rms(y) * gpost_ref[...])
        x1_ref[...] = x1
        h2 = (_rms(x1) * gpre_ref[...]) * (1 + scale2) + shift2
        h2_ref[...] = h2
        lg_ref[...] = jnp.dot(h2.astype(BF16), wr_ref[...], preferred_element_type=F32)

    i = pl.program_id(0)

    @pl.when(i < lat_tiles)
    def _():
        body(ml_ref, xl_ref)

    @pl.when(i >= lat_tiles)
    def _():
        body(mc_ref, xc_ref)


def _post(merged_lat, merged_ctx, x_lat, x_ctx, mods3, g_post, g_pre, w_o, w_router, *, mod_row, tm):
    n_lat, d = x_lat.shape
    t = n_lat + x_ctx.shape[0]
    e = w_router.shape[1]
    lat_tiles = n_lat // tm
    row = lambda i: (i, 0)
    lat = lambda i: (jnp.minimum(i, lat_tiles - 1), 0)
    ctx = lambda i: (jnp.maximum(i - lat_tiles, 0), 0)
    const = lambda i: (0, 0)
    return pl.pallas_call(
        functools.partial(_post_kernel, lat_tiles=lat_tiles),
        out_shape=(jax.ShapeDtypeStruct((t, d), F32),
                   jax.ShapeDtypeStruct((t, d), F32),
                   jax.ShapeDtypeStruct((t, e), F32)),
        grid=(t // tm,),
        in_specs=[pl.BlockSpec((tm, d), lat), pl.BlockSpec((tm, d), ctx),
                  pl.BlockSpec((tm, d), lat), pl.BlockSpec((tm, d), ctx),
                  pl.BlockSpec((1,) + mods3.shape[1:], lambda i: (mod_row(i), 0, 0)),
                  pl.BlockSpec((1, d), const), pl.BlockSpec((1, d), const),
                  pl.BlockSpec((d, d), const), pl.BlockSpec((d, e), const)],
        out_specs=(pl.BlockSpec((tm, d), row), pl.BlockSpec((tm, d), row),
                   pl.BlockSpec((tm, e), row)),
        compiler_params=_params("parallel"),
        name="post",
    )(merged_lat, merged_ctx, x_lat, x_ctx, mods3, g_post, g_pre, w_o, w_router)


def _route_kernel(lg_ref, bias_ref, idx_ref, w_ref, rank_ref, cnt_ref, carry_scr, below_scr):
    @pl.when(pl.program_id(0) == 0)
    def _():
        carry_scr[...] = jnp.zeros_like(carry_scr)
        r_i = lax.broadcasted_iota(I32, below_scr.shape, 0)
        c_i = lax.broadcasted_iota(I32, below_scr.shape, 1)
        below_scr[...] = jnp.where(c_i < r_i, 1.0, 0.0).astype(BF16)

    s = jax.nn.sigmoid(lg_ref[...])
    biased = s + bias_ref[...]
    tm, e = s.shape
    per = e // N_GROUPS
    lane = lax.broadcasted_iota(I32, (tm, e), 1)
    grp = lane // per
    neg = -jnp.inf

    def first_argmax(v):
        m = jnp.max(v, axis=-1, keepdims=True)
        idx = jnp.min(jnp.where(v == m, lane, e), axis=-1, keepdims=True)
        return m, idx

    scores = []
    for g in range(N_GROUPS):
        vg = jnp.where(grp == g, biased, neg)
        m1, i1 = first_argmax(vg)
        m2 = jnp.max(jnp.where(lane == i1, neg, vg), axis=-1, keepdims=True)
        scores.append(m1 + m2)
    group_score = jnp.zeros((tm, e), F32)
    for g in range(N_GROUPS):
        group_score = jnp.where(grp == g, scores[g], group_score)
    one_per_group = (lane % per) == 0
    allowed = jnp.zeros((tm, e), jnp.bool_)
    for g in range(N_GROUPS):
        wins = (group_score > scores[g]) | ((group_score == scores[g]) & (grp < g))
        beaten_by = jnp.sum(jnp.where(one_per_group & wins, 1.0, 0.0), axis=-1, keepdims=True)
        allowed = allowed | ((grp == g) & (beaten_by < TOPK_GROUPS))
    masked = jnp.where(allowed, biased, neg)
    hits, ids = [], []
    for _ in range(TOP_K):
        _, idx = first_argmax(masked)
        hit = lane == idx
        hits.append(hit)
        ids.append(idx)
        masked = jnp.where(hit, neg, masked)
    chosen = functools.reduce(jnp.logical_or, hits)
    w = jnp.where(chosen, s, 0.0)
    w = w / jnp.sum(w, axis=-1, keepdims=True) * ROUTED_SCALE

    onehot = jnp.where(chosen, 1.0, 0.0)
    rank = jnp.dot(below_scr[...], onehot.astype(BF16), preferred_element_type=F32) + carry_scr[...]
    carry = carry_scr[...] + jnp.sum(onehot, axis=0, keepdims=True)
    carry_scr[...] = carry
    cnt_ref[...] = carry.astype(I32)

    col = lax.broadcasted_iota(I32, (tm, TOP_K), 1)
    idx_o = jnp.zeros((tm, TOP_K), I32)
    w_o = jnp.zeros((tm, TOP_K), F32)
    rank_o = jnp.zeros((tm, TOP_K), F32)
    for k in range(TOP_K):
        wk = jnp.sum(jnp.where(hits[k], w, 0.0), axis=-1, keepdims=True)
        rk = jnp.sum(jnp.where(hits[k], rank, 0.0), axis=-1, keepdims=True)
        idx_o = jnp.where(col == k, ids[k], idx_o)
        w_o = jnp.where(col == k, wk, w_o)
        rank_o = jnp.where(col == k, rk, rank_o)
    idx_ref[...] = idx_o
    w_ref[...] = w_o
    rank_ref[...] = rank_o.astype(I32)


def _route(logits, bias, *, tm):
    t, e = logits.shape
    row = lambda i: (i, 0)
    return pl.pallas_call(
        _route_kernel,
        out_shape=(jax.ShapeDtypeStruct((t, TOP_K), I32), jax.ShapeDtypeStruct((t, TOP_K), F32),
                   jax.ShapeDtypeStruct((t, TOP_K), I32), jax.ShapeDtypeStruct((1, e), I32)),
        grid=(t // tm,),
        in_specs=[pl.BlockSpec((tm, e), row), pl.BlockSpec((1, e), lambda i: (0, 0))],
        out_specs=(pl.BlockSpec((tm, TOP_K), row), pl.BlockSpec((tm, TOP_K), row),
                   pl.BlockSpec((tm, TOP_K), row), pl.BlockSpec((1, e), lambda i: (0, 0))),
        scratch_shapes=[pltpu.VMEM((1, e), F32), pltpu.VMEM((tm, tm), BF16)],
        compiler_params=_params("arbitrary"),
        name="route",
    )(logits, bias)


def _slots_kernel(idx_ref, rank_ref, start_ref, o_ref):
    idx = idx_ref[...]
    tm = idx.shape[0]
    e = start_ref.shape[1]
    lane = lax.broadcasted_iota(I32, (tm, e), 1)
    col = lax.broadcasted_iota(I32, (tm, TOP_K), 1)
    base = jnp.zeros((tm, TOP_K), I32)
    for k in range(TOP_K):
        sk = jnp.sum(jnp.where(lane == idx[:, k:k + 1], start_ref[...], 0), axis=-1, keepdims=True)
        base = jnp.where(col == k, sk, base)
    o_ref[...] = base + rank_ref[...]


def _slots(idx, rank, start, *, tm):
    t = idx.shape[0]
    row = lambda i: (i, 0)
    return pl.pallas_call(
        _slots_kernel,
        out_shape=jax.ShapeDtypeStruct((t, TOP_K), I32),
        grid=(t // tm,),
        in_specs=[pl.BlockSpec((tm, TOP_K), row), pl.BlockSpec((tm, TOP_K), row),
                  pl.BlockSpec(start.shape, lambda i: (0, 0))],
        out_specs=pl.BlockSpec((tm, TOP_K), row),
        compiler_params=_params("parallel"),
        name="slots",
    )(idx, rank, start)


def _row_copy(src_ref, src_row, dst_ref, dst_row, sem):
    return pltpu.make_async_copy(src_ref.at[pl.ds(src_row, 1)], dst_ref.at[pl.ds(dst_row, 1)], sem)


def _dispatch_kernel(last_ref, used_ref, nv_ref, pos_ref, x_ref, xs_ref, zbuf, sem, zsem):
    tm = x_ref.shape[0]
    n_tiles = xs_ref.shape[0] // EXPERT_TILE

    def zero_tile(row):
        return pltpu.make_async_copy(zbuf, xs_ref.at[pl.ds(row, EXPERT_TILE)], zsem)

    def for_each_partly_empty_tile(fn):
        def per_expert(e, carry):
            @pl.when(used_ref[e] > 0)
            def _():
                fn(pl.multiple_of(last_ref[e], EXPERT_TILE))
            return carry

        def per_tile(j, carry):
            fn(pl.multiple_of(j * EXPERT_TILE, EXPERT_TILE))
            return carry

        lax.fori_loop(0, N_EXPERTS, per_expert, 0)
        lax.fori_loop(nv_ref[0], n_tiles, per_tile, 0)

    @pl.when(pl.program_id(0) == 0)
    def _():
        zbuf[...] = jnp.zeros_like(zbuf)
        for_each_partly_empty_tile(lambda row: zero_tile(row).start())
        for_each_partly_empty_tile(lambda row: zero_tile(row).wait())

    def body(r, carry):
        for k in range(TOP_K):
            _row_copy(x_ref, r, xs_ref, pos_ref[r * TOP_K + k], sem).start()
        return carry

    lax.fori_loop(0, tm, body, 0)
    n = tm * TOP_K
    pltpu.make_async_copy(xs_ref.at[pl.ds(0, n)], xs_ref.at[pl.ds(0, n)], sem).wait()


def _dispatch(last_row, used, n_valid, pos_flat, h2u, n_rows, *, tm):
    t, dw = h2u.shape
    return pl.pallas_call(
        _dispatch_kernel,
        out_shape=jax.ShapeDtypeStruct((n_rows, dw), F32),
        grid_spec=pltpu.PrefetchScalarGridSpec(
            num_scalar_prefetch=3,
            grid=(t // tm,),
            in_specs=[pl.BlockSpec((tm * TOP_K,), lambda i, *_: (i,), memory_space=pltpu.SMEM),
                      pl.BlockSpec((tm, dw), lambda i, *_: (i, 0))],
            out_specs=pl.BlockSpec(memory_space=pl.ANY),
            scratch_shapes=[pltpu.VMEM((EXPERT_TILE, dw), F32),
                            pltpu.SemaphoreType.DMA, pltpu.SemaphoreType.DMA]),
        compiler_params=_params("arbitrary"),
        name="dispatch",
    )(last_row, used, n_valid, pos_flat, h2u)


def _ffn(x, wg, wu, wd):
    a = jnp.dot(x, wg, preferred_element_type=F32)
    b = jnp.dot(x, wu, preferred_element_type=F32)
    hid = (a * jax.nn.sigmoid(a)) * b
    return jnp.dot(hid.astype(BF16), wd, preferred_element_type=F32)


def _expert_kernel(te_ref, nv_ref, x_ref, wg_ref, wu_ref, wd_ref, o_ref, wg_b, wu_b, wd_b):
    j = pl.program_id(0)
    live = j < nv_ref[0]
    new_expert = (j == 0) | (te_ref[j] != te_ref[jnp.maximum(j - 1, 0)])

    @pl.when(live & new_expert)
    def _():
        wg_b[...] = wg_ref[0].astype(BF16)
        wu_b[...] = wu_ref[0].astype(BF16)
        wd_b[...] = wd_ref[0].astype(BF16)

    @pl.when(live)
    def _():
        o_ref[...] = _ffn(x_ref[...].astype(BF16), wg_b[...], wu_b[...], wd_b[...])

    @pl.when(jnp.logical_not(live))
    def _():
        o_ref[...] = jnp.zeros_like(o_ref)


def _experts(tile_expert, n_valid, xs, wg, wu, wd):
    n_rows, dw = xs.shape
    ne, d, de = wg.shape
    tm = EXPERT_TILE
    row = lambda j, te, nv: (jnp.minimum(j, nv[0] - 1), 0)
    wsel = lambda j, te, nv: (te[j], 0, 0)
    return pl.pallas_call(
        _expert_kernel,
        out_shape=jax.ShapeDtypeStruct((n_rows, dw), F32),
        grid_spec=pltpu.PrefetchScalarGridSpec(
            num_scalar_prefetch=2,
            grid=(n_rows // tm,),
            in_specs=[pl.BlockSpec((tm, dw), row),
                      pl.BlockSpec((1, d, de), wsel), pl.BlockSpec((1, d, de), wsel),
                      pl.BlockSpec((1, de, d), wsel)],
            out_specs=pl.BlockSpec((tm, dw), lambda j, te, nv: (j, 0)),
            scratch_shapes=[pltpu.VMEM((d, de), BF16), pltpu.VMEM((d, de), BF16),
                            pltpu.VMEM((de, d), BF16)]),
        compiler_params=_params("arbitrary"),
        name="experts",
    )(tile_expert, n_valid, xs, wg, wu, wd)


def _combine_kernel(pos_ref, posn_ref, w_ref, h_ref, x1_ref, mod_ref, gpost_ref,
                    sg_ref, su_ref, sd_ref, ys_ref, olat_ref, octx_ref,
                    buf_even, buf_odd, sem_even, sem_odd, *, lat_tiles):
    tm = h_ref.shape[0]
    n = tm * TOP_K
    i = pl.program_id(0)
    last = pl.num_programs(0) - 1

    def drain(buf, sem):
        pltpu.make_async_copy(ys_ref.at[pl.ds(0, n)], buf, sem).wait()

    @pl.when(i == 0)
    def _():
        def body(r, carry):
            for k in range(TOP_K):
                _row_copy(ys_ref, pos_ref[r * TOP_K + k], buf_even, k * tm + r, sem_even).start()
            return carry
        lax.fori_loop(0, tm, body, 0)

    def step(buf, sem, buf_next, sem_next):
        drain(buf, sem)
        for r in range(tm):
            for k in range(TOP_K):
                _row_copy(ys_ref, posn_ref[r * TOP_K + k], buf_next, k * tm + r, sem_next).start()
        acc = _ffn(h_ref[...].astype(BF16), sg_ref[...], su_ref[...], sd_ref[...])
        w = w_ref[...]
        for k in range(TOP_K):
            acc = acc + w[:, k:k + 1] * buf[k * tm:(k + 1) * tm, :]
        gate2 = mod_ref[0, 5:6, :]
        res = x1_ref[...] + gate2 * (_rms(acc) * gpost_ref[...])

        @pl.when(i < lat_tiles)
        def _():
            olat_ref[...] = res

        @pl.when(i >= lat_tiles)
        def _():
            octx_ref[...] = res

        @pl.when(i == last)
        def _():
            drain(buf_next, sem_next)

    @pl.when(i % 2 == 0)
    def _():
        step(buf_even, sem_even, buf_odd, sem_odd)

    @pl.when(i % 2 == 1)
    def _():
        step(buf_odd, sem_odd, buf_even, sem_even)


def _combine(pos_flat, w, h2u, x1, mods3, g_post, sg, su, sd, ys, *, mod_row, tm, n_lat):
    t, d = x1.shape
    dw = h2u.shape[1]
    ds = sg.shape[1]
    steps = t // tm
    lat_tiles = n_lat // tm
    row = lambda i: (i, 0)
    const = lambda i: (0, 0)
    buf = pltpu.VMEM((tm * TOP_K, dw), F32)
    return pl.pallas_call(
        functools.partial(_combine_kernel, lat_tiles=lat_tiles),
        out_shape=(jax.ShapeDtypeStruct((n_lat, d), F32), jax.ShapeDtypeStruct((t - n_lat, d), F32)),
        grid=(steps,),
        in_specs=[pl.BlockSpec((tm * TOP_K,), lambda i: (i,), memory_space=pltpu.SMEM),
                  pl.BlockSpec((tm * TOP_K,), lambda i: (jnp.minimum(i + 1, steps - 1),),
                               memory_space=pltpu.SMEM),
                  pl.BlockSpec((tm, TOP_K), row), pl.BlockSpec((tm, dw), row),
                  pl.BlockSpec((tm, d), row),
                  pl.BlockSpec((1,) + mods3.shape[1:], lambda i: (mod_row(i), 0, 0)),
                  pl.BlockSpec((1, d), const),
                  pl.BlockSpec((d, ds), const), pl.BlockSpec((d, ds), const),
                  pl.BlockSpec((ds, d), const),
                  pl.BlockSpec(memory_space=pl.ANY)],
        out_specs=(pl.BlockSpec((tm, d), lambda i: (jnp.minimum(i, lat_tiles - 1), 0)),
                   pl.BlockSpec((tm, d), lambda i: (jnp.maximum(i - lat_tiles, 0), 0))),
        scratch_shapes=[buf, buf, pltpu.SemaphoreType.DMA, pltpu.SemaphoreType.DMA],
        compiler_params=_params("arbitrary"),
        name="combine",
    )(pos_flat, pos_flat, w, h2u, x1, mods3, g_post, sg, su, sd, ys)


def _rope_tables(n_tokens):
    rows = n_tokens // GRID_W
    row = jnp.repeat(jnp.arange(rows, dtype=F32), GRID_W)
    col = jnp.tile(jnp.arange(GRID_W, dtype=F32), rows)
    n_freq = HEAD_DIM // 4
    freqs = ROPE_THETA ** (-jnp.arange(n_freq, dtype=F32) / n_freq)
    ang_r = row[:, None] * freqs
    ang_c = col[:, None] * freqs
    ang = jnp.concatenate([ang_r, ang_r, ang_c, ang_c], axis=-1)
    sign = jnp.where((jnp.arange(HEAD_DIM) % (HEAD_DIM // 2)) < HEAD_DIM // 4, -1.0, 1.0)
    return jnp.cos(ang), jnp.sin(ang) * sign


def _mixer(x3, mods3, mod_row, wts, rope_tabs, ctx, *, proj_dtype, tm, tq):
    b, n, d = x3.shape
    x = x3.reshape(b * n, d)
    qk_cols = 4 * N_HEADS * HEAD_DIM
    cols = {"cb": qk_cols + N_HEADS * V_HEAD_DIM}
    cols["cc"] = cols["cb"] + D_CONV
    cols["cx"] = cols["cc"] + D_CONV
    cols["ga"] = cols["cx"] + D_CONV
    cols["gc"] = cols["ga"] + d
    proj = _proj(x, mods3, wts["g_pre_mix"], wts["w_in"], mod_row=mod_row(PROJ_TILE[0]),
                 rope_tabs=rope_tabs, out_dtype=proj_dtype, gate_col=cols["ga"], rope_cols=qk_cols,
                 tm=PROJ_TILE[0], tn=PROJ_TILE[1])
    attn = _attention(proj.reshape(b, n, -1), wts["lam_vecs"], wts["g_subln"], ctx, tq=tq)
    merged = _merge(attn.reshape(b * n, -1), proj, wts["conv_w"], wts["w_attn_out"],
                    wts["w_conv_out"], seq_len=n, cols=cols, tm=tm)
    return merged, x, proj


def _moe(x1, h2u, logits, mods3, mod_row, wts, *, n_lat):
    t = x1.shape[0]
    idx, w, rank, counts = _route(logits, wts["router_bias"], tm=ROUTE_TILE)
    tiles_per = (counts[0] + EXPERT_TILE - 1) // EXPERT_TILE
    tile_end = jnp.cumsum(tiles_per)
    start = ((tile_end - tiles_per) * EXPERT_TILE)[None].astype(I32)
    n_tiles = t * TOP_K // EXPERT_TILE + N_EXPERTS
    tile_expert = jnp.minimum(
        jnp.sum(jnp.arange(n_tiles, dtype=I32)[:, None] >= tile_end[None, :], axis=1),
        N_EXPERTS - 1).astype(I32)
    n_valid = tile_end[-1:].astype(I32)
    last_row = ((tile_end - 1) * EXPERT_TILE).astype(I32)
    used = (tiles_per > 0).astype(I32)
    pos = _slots(idx, rank, start, tm=ROUTE_TILE).reshape(t * TOP_K)
    xs = _dispatch(last_row, used, n_valid, pos, h2u, n_tiles * EXPERT_TILE, tm=DISPATCH_TILE)
    ys = _experts(tile_expert, n_valid, xs, wts["w_exp_gate"], wts["w_exp_up"], wts["w_exp_down"])
    return _combine(pos, w, h2u, x1, mods3, wts["g_post_ffn"], wts["w_sh_gate"], wts["w_sh_up"],
                    wts["w_sh_down"], ys, mod_row=mod_row(COMBINE_TILE), tm=COMBINE_TILE,
                    n_lat=n_lat)


def kernel(x_prompt, x_sample, cache_k, cache_v, c, c_ctx, w_ada, b_ada, g_pre_mix, g_post_mix, w_in, lambda_q1, lambda_k1, lambda_q2, lambda_k2, g_subln, conv_w, w_attn_out, w_conv_out, w_o, g_pre_ffn, g_post_ffn, w_router, router_bias, w_exp_gate, w_exp_up, w_exp_down, w_sh_gate, w_sh_up, w_sh_down):
    bp, sp, d = x_prompt.shape
    bs, ss, _ = x_sample.shape
    past = cache_k.shape[2]
    l = 0
    wts = {
        "g_pre_mix": g_pre_mix[l][None], "g_post_mix": g_post_mix[l][None],
        "g_pre_ffn": g_pre_ffn[l][None], "g_post_ffn": g_post_ffn[l][None],
        "g_subln": g_subln[l][None], "conv_w": conv_w[l],
        "lam_vecs": jnp.stack([lambda_q1[l], lambda_k1[l], lambda_q2[l], lambda_k2[l]]),
        "router_bias": router_bias[l][None],
        "w_in": w_in[l].astype(BF16), "w_attn_out": w_attn_out[l].astype(BF16),
        "w_conv_out": w_conv_out[l].astype(BF16), "w_o": w_o[l].astype(BF16),
        "w_router": w_router[l].astype(BF16),
        "w_exp_gate": w_exp_gate[l], "w_exp_up": w_exp_up[l], "w_exp_down": w_exp_down[l],
        "w_sh_gate": w_sh_gate[l].astype(BF16), "w_sh_up": w_sh_up[l].astype(BF16),
        "w_sh_down": w_sh_down[l].astype(BF16),
    }
    pad_rows = 8 - (bs + 1)
    c_rows = jnp.concatenate([c, c_ctx[None], jnp.zeros((pad_rows, d), F32)], axis=0)
    mods3 = _ada(c_rows, w_ada[l], b_ada[l][None]).reshape(8, 6, d)

    tm = ROW_TILE
    ctx_row = lambda tm_: (lambda i: bs)
    lat_row = lambda tm_: (lambda i: (i * tm_) // ss)
    all_row = lambda tm_: (lambda i: jnp.minimum((i * tm_) // ss, bs))
    merged_p, xp, proj_p = _mixer(x_prompt, mods3, ctx_row, wts, None, None,
                                  proj_dtype=F32, tm=tm, tq=sp)
    hk = 2 * N_HEADS * HEAD_DIM
    new_k = proj_p[:, hk:2 * hk].reshape(bp, 1, sp, 2, N_HEADS, HEAD_DIM)
    new_v = proj_p[:, 2 * hk:2 * hk + N_HEADS * V_HEAD_DIM].reshape(bp, 1, sp, N_HEADS, V_HEAD_DIM)

    ctx = (cache_k[:, l].reshape(bs, past, hk).astype(BF16),
           cache_v[:, l].reshape(bs, past, N_HEADS * V_HEAD_DIM).astype(BF16))
    merged_s, xs, _ = _mixer(x_sample, mods3, lat_row, wts, _rope_tables(ss), ctx,
                             proj_dtype=BF16, tm=tm, tq=QUERY_TILE)

    n_lat = bs * ss
    x1, h2, logits = _post(merged_s, merged_p, xs, xp, mods3, wts["g_post_mix"], wts["g_pre_ffn"],
                           wts["w_o"], wts["w_router"], mod_row=all_row(POST_TILE), tm=POST_TILE)
    ys, yp = _moe(x1, h2, logits, mods3, all_row, wts, n_lat=n_lat)
    return (yp.reshape(bp, sp, d), ys.reshape(bs, ss, d), new_k, new_v)
```

```python
import functools
import math

import jax
import jax.numpy as jnp
from jax import lax
from jax.experimental import pallas as pl
from jax.experimental.pallas import tpu as pltpu

F32 = jnp.float32
BF16 = jnp.bfloat16
I32 = jnp.int32

EPS = 1e-6
N_HEADS = 8
HEAD_DIM = 128
V_HEAD_DIM = 2 * HEAD_DIM
D_CONV = 1024
N_EXPERTS = 64
TOP_K = 8
N_GROUPS = 8
TOPK_GROUPS = 4
ROUTED_SCALE = 2.5
ROPE_THETA = 10000.0
GRID_W = 64
LAM_INIT = 0.8 - 0.6 * math.exp(-0.3 * 0)

LANES = 128
HALO_ROWS = 16
VMEM_LIMIT_BYTES = 56 * 1024 * 1024
EXPERT_TILE = 256
KEY_CHUNK = 512
PROJ_SUB = 256
QUERY_TILE = 512
PROJ_TILE = (1024, 1024)
ROW_TILE = 512
POST_TILE = 256
DISPATCH_TILE = 256
COMBINE_TILE = 128
ROUTE_TILE = 512


def _params(*sem):
    return pltpu.CompilerParams(dimension_semantics=sem, vmem_limit_bytes=VMEM_LIMIT_BYTES)


def _rms(x):
    return x * lax.rsqrt(jnp.mean(x * x, axis=-1, keepdims=True) + EPS)


def _ada_kernel(c_ref, w_ref, b_ref, o_ref):
    c = c_ref[...]
    a = (c * jax.nn.sigmoid(c)).astype(BF16)
    o_ref[...] = jnp.dot(a, w_ref[...].astype(BF16), preferred_element_type=F32) + b_ref[...]


def _ada(c_rows, w_ada, b_ada, tn=1024):
    r, d = c_rows.shape
    n = w_ada.shape[1]
    return pl.pallas_call(
        _ada_kernel,
        out_shape=jax.ShapeDtypeStruct((r, n), F32),
        grid=(n // tn,),
        in_specs=[pl.BlockSpec((r, d), lambda j: (0, 0)),
                  pl.BlockSpec((d, tn), lambda j: (0, j)),
                  pl.BlockSpec((1, tn), lambda j: (0, j))],
        out_specs=pl.BlockSpec((r, tn), lambda j: (0, j)),
        compiler_params=_params("arbitrary"),
        name="ada",
    )(c_rows, w_ada, b_ada)


def _proj_kernel(*refs, rope, n_rope_tiles, sig_start):
    if rope:
        x_ref, mod_ref, g_ref, w_ref, cos_ref, sin_ref, o_ref, h_scr = refs
    else:
        x_ref, mod_ref, g_ref, w_ref, o_ref, h_scr = refs
    j = pl.program_id(1)

    @pl.when(j == 0)
    def _():
        shift = mod_ref[0, 0:1, :]
        scale = mod_ref[0, 1:2, :]
        h = (_rms(x_ref[...]) * g_ref[...]) * (1 + scale) + shift
        h_scr[...] = h.astype(BF16)

    tn = w_ref.shape[1]

    def pieces():
        for lo in range(0, tn, PROJ_SUB):
            yield lo, jnp.dot(h_scr[...], w_ref[:, lo:lo + PROJ_SUB], preferred_element_type=F32)

    if rope:
        @pl.when(j < n_rope_tiles)
        def _():
            cos = cos_ref[...]
            sin = sin_ref[...]
            lane = lax.broadcasted_iota(jnp.int32, cos.shape, 1)
            first = (lane % (HEAD_DIM // 2)) < (HEAD_DIM // 4)
            for lo, acc in pieces():
                for s in range(PROJ_SUB // HEAD_DIM):
                    xs = acc[:, s * HEAD_DIM:(s + 1) * HEAD_DIM]
                    partner = jnp.where(first,
                                        pltpu.roll(xs, HEAD_DIM - HEAD_DIM // 4, axis=1),
                                        pltpu.roll(xs, HEAD_DIM // 4, axis=1))
                    c0 = lo + s * HEAD_DIM
                    o_ref[:, c0:c0 + HEAD_DIM] = (xs * cos + partner * sin).astype(o_ref.dtype)
        plain_lo = n_rope_tiles
    else:
        plain_lo = 0

    @pl.when((j >= plain_lo) & (j < sig_start))
    def _():
        for lo, acc in pieces():
            o_ref[:, lo:lo + PROJ_SUB] = acc.astype(o_ref.dtype)

    @pl.when(j >= sig_start)
    def _():
        for lo, acc in pieces():
            o_ref[:, lo:lo + PROJ_SUB] = jax.nn.sigmoid(acc).astype(o_ref.dtype)


def _proj(x, mods3, g, w, *, mod_row, rope_tabs, out_dtype, gate_col, rope_cols, tm, tn=512):
    t, d = x.shape
    n = w.shape[1]
    rope = rope_tabs is not None
    in_specs = [pl.BlockSpec((tm, d), lambda i, j: (i, 0)),
                pl.BlockSpec((1,) + mods3.shape[1:], lambda i, j: (mod_row(i), 0, 0)),
                pl.BlockSpec((1, d), lambda i, j: (0, 0)),
                pl.BlockSpec((d, tn), lambda i, j: (0, j))]
    args = [x, mods3, g, w]
    if rope:
        cos, sin = rope_tabs
        nblk = cos.shape[0] // tm
        in_specs += [pl.BlockSpec((tm, HEAD_DIM), lambda i, j: (i % nblk, 0)),
                     pl.BlockSpec((tm, HEAD_DIM), lambda i, j: (i % nblk, 0))]
        args += [cos, sin]
    kern = functools.partial(_proj_kernel, rope=rope, n_rope_tiles=rope_cols // tn,
                             sig_start=gate_col // tn)
    return pl.pallas_call(
        kern,
        out_shape=jax.ShapeDtypeStruct((t, n), out_dtype),
        grid=(t // tm, n // tn),
        in_specs=in_specs,
        out_specs=pl.BlockSpec((tm, tn), lambda i, j: (i, j)),
        scratch_shapes=[pltpu.VMEM((tm, d), BF16)],
        compiler_params=_params("parallel", "arbitrary"),
        name="proj_rope" if rope else "proj",
    )(*args)


def _attn_kernel(*refs, has_ctx):
    if has_ctx:
        (lam_ref, gs_ref, q1_ref, q2_ref, k1_ref, k2_ref, v_ref,
         k1c_ref, k2c_ref, vc_ref, o_ref, s_even, s_odd, m_even, m_odd) = refs
    else:
        (lam_ref, gs_ref, q1_ref, q2_ref, k1_ref, k2_ref, v_ref,
         o_ref, s_even, s_odd, m_even, m_odd) = refs
    t = pl.program_id(1)
    lv = lam_ref[...]
    lam = (jnp.exp(jnp.sum(lv[0:1] * lv[1:2], axis=-1, keepdims=True))
           - jnp.exp(jnp.sum(lv[2:3] * lv[3:4], axis=-1, keepdims=True)) + LAM_INIT)
    c = (HEAD_DIM ** -0.5) * math.log2(math.e)
    nt = (((1,), (1,)), ((), ()))
    tq = q1_ref.shape[1]

    chunks = []
    off = 0
    groups = [(k1_ref, k2_ref, v_ref)] + ([(k1c_ref, k2c_ref, vc_ref)] if has_ctx else [])
    for ka, kb, vv in groups:
        n = ka.shape[1]
        ch = min(KEY_CHUNK, n)
        for c0 in range(0, n, ch):
            chunks.append((ka, kb, vv, c0, ch, off + c0))
        off += n

    @pl.when(t == 0)
    def _():
        s_odd[...] = jnp.zeros_like(s_odd)
        m_odd[...] = jnp.zeros_like(m_odd)

    def step(s_fill, m_fill, s_drain, m_drain):
        qs = (q1_ref[0].astype(BF16), q2_ref[0].astype(BF16))
        mbs = (m_drain[0], m_drain[1])
        mrun = [jnp.full((tq, LANES), -jnp.inf, F32)] * 2
        lrun = [jnp.zeros((tq, LANES), F32)] * 2
        acc = [jnp.zeros((tq, V_HEAD_DIM), F32)] * 2
        for ka, kb, vv, c0, ch, col in chunks:
            for mp in range(2):
                k = (ka, kb)[mp][0, c0:c0 + ch, :].astype(BF16)
                s = lax.dot_general(qs[mp], k, nt, preferred_element_type=F32)
                s_fill[mp, :, col:col + ch] = s
                for g in range(ch // LANES):
                    mrun[mp] = jnp.maximum(mrun[mp], s[:, g * LANES:(g + 1) * LANES])
            v = vv[0, c0:c0 + ch, :].astype(BF16)
            for mp in range(2):
                ps = []
                for g in range(ch // LANES):
                    sg = s_drain[mp, :, col + g * LANES:col + (g + 1) * LANES]
                    pg = jnp.exp2((sg - mbs[mp]) * c)
                    lrun[mp] = lrun[mp] + pg
                    ps.append(pg.astype(BF16))
                acc[mp] = acc[mp] + jnp.dot(jnp.concatenate(ps, axis=1), v, preferred_element_type=F32)
        for mp in range(2):
            m_fill[mp] = jnp.broadcast_to(jnp.max(mrun[mp], axis=-1, keepdims=True), (tq, LANES))
        l1 = jnp.sum(lrun[0], axis=-1, keepdims=True)
        l2 = jnp.sum(lrun[1], axis=-1, keepdims=True)
        o = acc[0] * (1.0 / l1) - acc[1] * (lam / l2)
        o_ref[0] = ((_rms(o) * gs_ref[...]) * (1 - LAM_INIT)).astype(o_ref.dtype)

    @pl.when(t % 2 == 0)
    def _():
        step(s_even, m_even, s_odd, m_odd)

    @pl.when(t % 2 == 1)
    def _():
        step(s_odd, m_odd, s_even, m_even)


def _attention(proj3, lam_vecs, g_subln, ctx, *, tq):
    b, n, _ = proj3.shape
    h = N_HEADS
    nq = n // tq
    n_tiles = h * nq
    has_ctx = ctx is not None
    f_head = lambda t: jnp.minimum(t, n_tiles - 1) // nq
    f_row = lambda t: jnp.minimum(t, n_tiles - 1) % nq
    d_head = lambda t: jnp.maximum(t - 1, 0) // nq
    d_row = lambda t: jnp.maximum(t - 1, 0) % nq
    q_spec = lambda off: pl.BlockSpec((1, tq, HEAD_DIM), lambda bi, t: (bi, f_row(t), off + f_head(t)))
    k_spec = lambda rows, off: pl.BlockSpec((1, rows, HEAD_DIM), lambda bi, t: (bi, 0, off + f_head(t)))
    v_spec = lambda rows, off: pl.BlockSpec((1, rows, V_HEAD_DIM), lambda bi, t: (bi, 0, off + d_head(t)))
    in_specs = [pl.BlockSpec((4, HEAD_DIM), lambda bi, t: (0, 0)),
                pl.BlockSpec((1, V_HEAD_DIM), lambda bi, t: (0, 0)),
                q_spec(0), q_spec(h), k_spec(n, 2 * h), k_spec(n, 3 * h), v_spec(n, 2 * h)]
    args = [lam_vecs, g_subln, proj3, proj3, proj3, proj3, proj3]
    n_keys = n
    if has_ctx:
        ck, cv = ctx
        p = ck.shape[1]
        n_keys += p
        in_specs += [k_spec(p, 0), k_spec(p, h), v_spec(p, 0)]
        args += [ck, ck, cv]
    scores = pltpu.VMEM((2, tq, n_keys), F32)
    maxima = pltpu.VMEM((2, tq, LANES), F32)
    return pl.pallas_call(
        functools.partial(_attn_kernel, has_ctx=has_ctx),
        out_shape=jax.ShapeDtypeStruct((b, n, h * V_HEAD_DIM), BF16),
        grid=(b, n_tiles + 1),
        in_specs=in_specs,
        out_specs=pl.BlockSpec((1, tq, V_HEAD_DIM), lambda bi, t: (bi, d_row(t), d_head(t))),
        scratch_shapes=[scores, scores, maxima, maxima],
        compiler_params=_params("parallel", "arbitrary"),
        name="attn_ctx" if has_ctx else "attn",
    )(*args)


def _merge_kernel(attn_ref, cb_ref, cc_ref, cx_ref, ccp_ref, cxp_ref, ccn_ref, cxn_ref,
                  cw_ref, wa_ref, wc_ref, ga_ref, gc_ref, o_ref, conv_scr, *, seq_len):
    i = pl.program_id(0)
    j = pl.program_id(1)
    tm = cc_ref.shape[0]

    @pl.when(j == 0)
    def _():
        u = cc_ref[...].astype(F32) * cx_ref[...].astype(F32)
        u_before = (ccp_ref[...].astype(F32) * cxp_ref[...].astype(F32))[HALO_ROWS - 1:HALO_ROWS, :]
        u_after = (ccn_ref[...].astype(F32) * cxn_ref[...].astype(F32))[0:1, :]
        row = lax.broadcasted_iota(jnp.int32, (tm, 1), 0)
        pos = (i * tm + row) % seq_len
        u_prev = jnp.where(row == 0, u_before, pltpu.roll(u, 1, axis=0))
        u_prev = jnp.where(pos == 0, 0.0, u_prev)
        u_next = jnp.where(row == tm - 1, u_after, pltpu.roll(u, tm - 1, axis=0))
        u_next = jnp.where(pos == seq_len - 1, 0.0, u_next)
        cw = cw_ref[...]
        conv = u_prev * cw[0:1, :] + u * cw[1:2, :] + u_next * cw[2:3, :]
        conv_scr[...] = (cb_ref[...].astype(F32) * conv).astype(BF16)

    ya = jnp.dot(attn_ref[...], wa_ref[...], preferred_element_type=F32)
    yc = jnp.dot(conv_scr[...], wc_ref[...], preferred_element_type=F32)
    o_ref[...] = (ga_ref[...].astype(F32) * ya + gc_ref[...].astype(F32) * yc).astype(o_ref.dtype)


def _merge(attn, proj, conv_w, w_attn_out, w_conv_out, *, seq_len, cols, tm, tn=1024):
    t, aw = attn.shape
    d = w_attn_out.shape[1]
    cb0, cc0, cx0, ga0, gc0 = (cols[k] for k in ("cb", "cc", "cx", "ga", "gc"))
    hb = tm // HALO_ROWS
    last_halo = t // HALO_ROWS - 1
    cblk = lambda c0: pl.BlockSpec((tm, D_CONV), lambda i, j: (i, c0 // D_CONV))
    halo_prev = lambda c0: pl.BlockSpec(
        (HALO_ROWS, D_CONV), lambda i, j: (jnp.maximum(i * hb - 1, 0), c0 // D_CONV))
    halo_next = lambda c0: pl.BlockSpec(
        (HALO_ROWS, D_CONV), lambda i, j: (jnp.minimum((i + 1) * hb, last_halo), c0 // D_CONV))
    in_specs = [pl.BlockSpec((tm, aw), lambda i, j: (i, 0)),
                cblk(cb0), cblk(cc0), cblk(cx0),
                halo_prev(cc0), halo_prev(cx0), halo_next(cc0), halo_next(cx0),
                pl.BlockSpec(conv_w.shape, lambda i, j: (0, 0)),
                pl.BlockSpec((aw, tn), lambda i, j: (0, j)),
                pl.BlockSpec((D_CONV, tn), lambda i, j: (0, j)),
                pl.BlockSpec((tm, tn), lambda i, j: (i, ga0 // tn + j)),
                pl.BlockSpec((tm, tn), lambda i, j: (i, gc0 // tn + j))]
    return pl.pallas_call(
        functools.partial(_merge_kernel, seq_len=seq_len),
        out_shape=jax.ShapeDtypeStruct((t, d), BF16),
        grid=(t // tm, d // tn),
        in_specs=in_specs,
        out_specs=pl.BlockSpec((tm, tn), lambda i, j: (i, j)),
        scratch_shapes=[pltpu.VMEM((tm, D_CONV), BF16)],
        compiler_params=_params("parallel", "arbitrary"),
        name="merge",
    )(attn, proj, proj, proj, proj, proj, proj, proj, conv_w, w_attn_out, w_conv_out, proj, proj)


def _post_kernel(ml_ref, mc_ref, xl_ref, xc_ref, mod_ref, gpost_ref, gpre_ref, wo_ref, wr_ref,
                 x1_ref, h2_ref, lg_ref, *, lat_tiles):
    def body(m_ref, x_ref):
        y = jnp.dot(m_ref[...], wo_ref[...], preferred_element_type=F32)
        gate1 = mod_ref[0, 2:3, :]
        shift2 = mod_ref[0, 3:4, :]
        scale2 = mod_ref[0, 4:5, :]
        x1 = x_ref[...] + gate1 * (_rms(y) * gpost_ref[...])
        x1_ref[...] = x1
        h2 = (_rms(x1) * gpre_ref[...]) * (1 + scale2) + shift2
        h2_ref[...] = h2
        lg_ref[...] = jnp.dot(h2.astype(BF16), wr_ref[...], preferred_element_type=F32)

    i = pl.program_id(0)

    @pl.when(i < lat_tiles)
    def _():
        body(ml_ref, xl_ref)

    @pl.when(i >= lat_tiles)
    def _():
        body(mc_ref, xc_ref)


def _post(merged_lat, merged_ctx, x_lat, x_ctx, mods3, g_post, g_pre, w_o, w_router, *, mod_row, tm):
    n_lat, d = x_lat.shape
    t = n_lat + x_ctx.shape[0]
    e = w_router.shape[1]
    lat_tiles = n_lat // tm
    row = lambda i: (i, 0)
    lat = lambda i: (jnp.minimum(i, lat_tiles - 1), 0)
    ctx = lambda i: (jnp.maximum(i - lat_tiles, 0), 0)
    const = lambda i: (0, 0)
    return pl.pallas_call(
        functools.partial(_post_kernel, lat_tiles=lat_tiles),
        out_shape=(jax.ShapeDtypeStruct((t, d), F32),
                   jax.ShapeDtypeStruct((t, d), F32),
                   jax.ShapeDtypeStruct((t, e), F32)),
        grid=(t // tm,),
        in_specs=[pl.BlockSpec((tm, d), lat), pl.BlockSpec((tm, d), ctx),
                  pl.BlockSpec((tm, d), lat), pl.BlockSpec((tm, d), ctx),
                  pl.BlockSpec((1,) + mods3.shape[1:], lambda i: (mod_row(i), 0, 0)),
                  pl.BlockSpec((1, d), const), pl.BlockSpec((1, d), const),
                  pl.BlockSpec((d, d), const), pl.BlockSpec((d, e), const)],
        out_specs=(pl.BlockSpec((tm, d), row), pl.BlockSpec((tm, d), row),
                   pl.BlockSpec((tm, e), row)),
        compiler_params=_params("parallel"),
        name="post",
    )(merged_lat, merged_ctx, x_lat, x_ctx, mods3, g_post, g_pre, w_o, w_router)


def _route_kernel(lg_ref, bias_ref, idx_ref, w_ref, rank_ref, cnt_ref, carry_scr, below_scr):
    @pl.when(pl.program_id(0) == 0)
    def _():
        carry_scr[...] = jnp.zeros_like(carry_scr)
        r_i = lax.broadcasted_iota(I32, below_scr.shape, 0)
        c_i = lax.broadcasted_iota(I32, below_scr.shape, 1)
        below_scr[...] = jnp.where(c_i < r_i, 1.0, 0.0).astype(BF16)

    s = jax.nn.sigmoid(lg_ref[...])
    biased = s + bias_ref[...]
    tm, e = s.shape
    per = e // N_GROUPS
    lane = lax.broadcasted_iota(I32, (tm, e), 1)
    grp = lane // per
    neg = -jnp.inf

    def first_argmax(v):
        m = jnp.max(v, axis=-1, keepdims=True)
        idx = jnp.min(jnp.where(v == m, lane, e), axis=-1, keepdims=True)
        return m, idx

    scores = []
    for g in range(N_GROUPS):
        vg = jnp.where(grp == g, biased, neg)
        m1, i1 = first_argmax(vg)
        m2 = jnp.max(jnp.where(lane == i1, neg, vg), axis=-1, keepdims=True)
        scores.append(m1 + m2)
    group_score = jnp.zeros((tm, e), F32)
    for g in range(N_GROUPS):
        group_score = jnp.where(grp == g, scores[g], group_score)
    one_per_group = (lane % per) == 0
    allowed = jnp.zeros((tm, e), jnp.bool_)
    for g in range(N_GROUPS):
        wins = (group_score > scores[g]) | ((group_score == scores[g]) & (grp < g))
        beaten_by = jnp.sum(jnp.where(one_per_group & wins, 1.0, 0.0), axis=-1, keepdims=True)
        allowed = allowed | ((grp == g) & (beaten_by < TOPK_GROUPS))
    masked = jnp.where(allowed, biased, neg)
    hits, ids = [], []
    for _ in range(TOP_K):
        _, idx = first_argmax(masked)
        hit = lane == idx
        hits.append(hit)
        ids.append(idx)
        masked = jnp.where(hit, neg, masked)
    chosen = functools.reduce(jnp.logical_or, hits)
    w = jnp.where(chosen, s, 0.0)
    w = w / jnp.sum(w, axis=-1, keepdims=True) * ROUTED_SCALE

    onehot = jnp.where(chosen, 1.0, 0.0)
    rank = jnp.dot(below_scr[...], onehot.astype(BF16), preferred_element_type=F32) + carry_scr[...]
    carry = carry_scr[...] + jnp.sum(onehot, axis=0, keepdims=True)
    carry_scr[...] = carry
    cnt_ref[...] = carry.astype(I32)

    col = lax.broadcasted_iota(I32, (tm, TOP_K), 1)
    idx_o = jnp.zeros((tm, TOP_K), I32)
    w_o = jnp.zeros((tm, TOP_K), F32)
    rank_o = jnp.zeros((tm, TOP_K), F32)
    for k in range(TOP_K):
        wk = jnp.sum(jnp.where(hits[k], w, 0.0), axis=-1, keepdims=True)
        rk = jnp.sum(jnp.where(hits[k], rank, 0.0), axis=-1, keepdims=True)
        idx_o = jnp.where(col == k, ids[k], idx_o)
        w_o = jnp.where(col == k, wk, w_o)
        rank_o = jnp.where(col == k, rk, rank_o)
    idx_ref[...] = idx_o
    w_ref[...] = w_o
    rank_ref[...] = rank_o.astype(I32)


def _route(logits, bias, *, tm):
    t, e = logits.shape
    row = lambda i: (i, 0)
    return pl.pallas_call(
        _route_kernel,
        out_shape=(jax.ShapeDtypeStruct((t, TOP_K), I32), jax.ShapeDtypeStruct((t, TOP_K), F32),
                   jax.ShapeDtypeStruct((t, TOP_K), I32), jax.ShapeDtypeStruct((1, e), I32)),
        grid=(t // tm,),
        in_specs=[pl.BlockSpec((tm, e), row), pl.BlockSpec((1, e), lambda i: (0, 0))],
        out_specs=(pl.BlockSpec((tm, TOP_K), row), pl.BlockSpec((tm, TOP_K), row),
                   pl.BlockSpec((tm, TOP_K), row), pl.BlockSpec((1, e), lambda i: (0, 0))),
        scratch_shapes=[pltpu.VMEM((1, e), F32), pltpu.VMEM((tm, tm), BF16)],
        compiler_params=_params("arbitrary"),
        name="route",
    )(logits, bias)


def _slots_kernel(idx_ref, rank_ref, start_ref, o_ref):
    idx = idx_ref[...]
    tm = idx.shape[0]
    e = start_ref.shape[1]
    lane = lax.broadcasted_iota(I32, (tm, e), 1)
    col = lax.broadcasted_iota(I32, (tm, TOP_K), 1)
    base = jnp.zeros((tm, TOP_K), I32)
    for k in range(TOP_K):
        sk = jnp.sum(jnp.where(lane == idx[:, k:k + 1], start_ref[...], 0), axis=-1, keepdims=True)
        base = jnp.where(col == k, sk, base)
    o_ref[...] = base + rank_ref[...]


def _slots(idx, rank, start, *, tm):
    t = idx.shape[0]
    row = lambda i: (i, 0)
    return pl.pallas_call(
        _slots_kernel,
        out_shape=jax.ShapeDtypeStruct((t, TOP_K), I32),
        grid=(t // tm,),
        in_specs=[pl.BlockSpec((tm, TOP_K), row), pl.BlockSpec((tm, TOP_K), row),
                  pl.BlockSpec(start.shape, lambda i: (0, 0))],
        out_specs=pl.BlockSpec((tm, TOP_K), row),
        compiler_params=_params("parallel"),
        name="slots",
    )(idx, rank, start)


def _row_copy(src_ref, src_row, dst_ref, dst_row, sem):
    return pltpu.make_async_copy(src_ref.at[pl.ds(src_row, 1)], dst_ref.at[pl.ds(dst_row, 1)], sem)


def _dispatch_kernel(last_ref, used_ref, nv_ref, pos_ref, x_ref, xs_ref, zbuf, sem, zsem):
    tm = x_ref.shape[0]
    n_tiles = xs_ref.shape[0] // EXPERT_TILE

    def zero_tile(row):
        return pltpu.make_async_copy(zbuf, xs_ref.at[pl.ds(row, EXPERT_TILE)], zsem)

    def for_each_partly_empty_tile(fn):
        def per_expert(e, carry):
            @pl.when(used_ref[e] > 0)
            def _():
                fn(pl.multiple_of(last_ref[e], EXPERT_TILE))
            return carry

        def per_tile(j, carry):
            fn(pl.multiple_of(j * EXPERT_TILE, EXPERT_TILE))
            return carry

        lax.fori_loop(0, N_EXPERTS, per_expert, 0)
        lax.fori_loop(nv_ref[0], n_tiles, per_tile, 0)

    @pl.when(pl.program_id(0) == 0)
    def _():
        zbuf[...] = jnp.zeros_like(zbuf)
        for_each_partly_empty_tile(lambda row: zero_tile(row).start())
        for_each_partly_empty_tile(lambda row: zero_tile(row).wait())

    def body(r, carry):
        for k in range(TOP_K):
            _row_copy(x_ref, r, xs_ref, pos_ref[r * TOP_K + k], sem).start(priority=k % 2)
        return carry

    lax.fori_loop(0, tm, body, 0)
    n = tm * TOP_K
    pltpu.make_async_copy(xs_ref.at[pl.ds(0, n)], xs_ref.at[pl.ds(0, n)], sem).wait()


def _dispatch(last_row, used, n_valid, pos_flat, h2u, n_rows, *, tm):
    t, dw = h2u.shape
    return pl.pallas_call(
        _dispatch_kernel,
        out_shape=jax.ShapeDtypeStruct((n_rows, dw), F32),
        grid_spec=pltpu.PrefetchScalarGridSpec(
            num_scalar_prefetch=3,
            grid=(t // tm,),
            in_specs=[pl.BlockSpec((tm * TOP_K,), lambda i, *_: (i,), memory_space=pltpu.SMEM),
                      pl.BlockSpec((tm, dw), lambda i, *_: (i, 0))],
            out_specs=pl.BlockSpec(memory_space=pl.ANY),
            scratch_shapes=[pltpu.VMEM((EXPERT_TILE, dw), F32),
                            pltpu.SemaphoreType.DMA, pltpu.SemaphoreType.DMA]),
        compiler_params=_params("arbitrary"),
        name="dispatch",
    )(last_row, used, n_valid, pos_flat, h2u)


def _ffn(x, wg, wu, wd):
    a = jnp.dot(x, wg, preferred_element_type=F32)
    b = jnp.dot(x, wu, preferred_element_type=F32)
    hid = (a * jax.nn.sigmoid(a)) * b
    return jnp.dot(hid.astype(BF16), wd, preferred_element_type=F32)


def _expert_kernel(te_ref, nv_ref, x_ref, wg_ref, wu_ref, wd_ref, o_ref, wg_b, wu_b, wd_b):
    j = pl.program_id(0)
    live = j < nv_ref[0]
    new_expert = (j == 0) | (te_ref[j] != te_ref[jnp.maximum(j - 1, 0)])

    @pl.when(live & new_expert)
    def _():
        wg_b[...] = wg_ref[0].astype(BF16)
        wu_b[...] = wu_ref[0].astype(BF16)
        wd_b[...] = wd_ref[0].astype(BF16)

    @pl.when(live)
    def _():
        o_ref[...] = _ffn(x_ref[...].astype(BF16), wg_b[...], wu_b[...], wd_b[...])

    @pl.when(jnp.logical_not(live))
    def _():
        o_ref[...] = jnp.zeros_like(o_ref)


def _experts(tile_expert, n_valid, xs, wg, wu, wd):
    n_rows, dw = xs.shape
    ne, d, de = wg.shape
    tm = EXPERT_TILE
    row = lambda j, te, nv: (jnp.minimum(j, nv[0] - 1), 0)
    wsel = lambda j, te, nv: (te[j], 0, 0)
    return pl.pallas_call(
        _expert_kernel,
        out_shape=jax.ShapeDtypeStruct((n_rows, dw), F32),
        grid_spec=pltpu.PrefetchScalarGridSpec(
            num_scalar_prefetch=2,
            grid=(n_rows // tm,),
            in_specs=[pl.BlockSpec((tm, dw), row),
                      pl.BlockSpec((1, d, de), wsel), pl.BlockSpec((1, d, de), wsel),
                      pl.BlockSpec((1, de, d), wsel)],
            out_specs=pl.BlockSpec((tm, dw), lambda j, te, nv: (j, 0)),
            scratch_shapes=[pltpu.VMEM((d, de), BF16), pltpu.VMEM((d, de), BF16),
                            pltpu.VMEM((de, d), BF16)]),
        compiler_params=_params("arbitrary"),
        name="experts",
    )(tile_expert, n_valid, xs, wg, wu, wd)


def _combine_kernel(pos_ref, posn_ref, w_ref, h_ref, x1_ref, mod_ref, gpost_ref,
                    sg_ref, su_ref, sd_ref, ys_ref, olat_ref, octx_ref,
                    buf_even, buf_odd, sem_even, sem_odd, *, lat_tiles):
    tm = h_ref.shape[0]
    n = tm * TOP_K
    i = pl.program_id(0)
    last = pl.num_programs(0) - 1

    def drain(buf, sem):
        pltpu.make_async_copy(ys_ref.at[pl.ds(0, n)], buf, sem).wait()

    @pl.when(i == 0)
    def _():
        def body(r, carry):
            for k in range(TOP_K):
                _row_copy(ys_ref, pos_ref[r * TOP_K + k], buf_even, k * tm + r,
                          sem_even).start(priority=k % 2)
            return carry
        lax.fori_loop(0, tm, body, 0)

    def step(buf, sem, buf_next, sem_next):
        drain(buf, sem)
        for r in range(tm):
            for k in range(TOP_K):
                _row_copy(ys_ref, posn_ref[r * TOP_K + k], buf_next, k * tm + r,
                          sem_next).start(priority=k % 2)
        acc = _ffn(h_ref[...].astype(BF16), sg_ref[...], su_ref[...], sd_ref[...])
        w = w_ref[...]
        for k in range(TOP_K):
            acc = acc + w[:, k:k + 1] * buf[k * tm:(k + 1) * tm, :]
        gate2 = mod_ref[0, 5:6, :]
        res = x1_ref[...] + gate2 * (_rms(acc) * gpost_ref[...])

        @pl.when(i < lat_tiles)
        def _():
            olat_ref[...] = res

        @pl.when(i >= lat_tiles)
        def _():
            octx_ref[...] = res

        @pl.when(i == last)
        def _():
            drain(buf_next, sem_next)

    @pl.when(i % 2 == 0)
    def _():
        step(buf_even, sem_even, buf_odd, sem_odd)

    @pl.when(i % 2 == 1)
    def _():
        step(buf_odd, sem_odd, buf_even, sem_even)


def _combine(pos_flat, w, h2u, x1, mods3, g_post, sg, su, sd, ys, *, mod_row, tm, n_lat):
    t, d = x1.shape
    dw = h2u.shape[1]
    ds = sg.shape[1]
    steps = t // tm
    lat_tiles = n_lat // tm
    row = lambda i: (i, 0)
    const = lambda i: (0, 0)
    buf = pltpu.VMEM((tm * TOP_K, dw), F32)
    return pl.pallas_call(
        functools.partial(_combine_kernel, lat_tiles=lat_tiles),
        out_shape=(jax.ShapeDtypeStruct((n_lat, d), F32), jax.ShapeDtypeStruct((t - n_lat, d), F32)),
        grid=(steps,),
        in_specs=[pl.BlockSpec((tm * TOP_K,), lambda i: (i,), memory_space=pltpu.SMEM),
                  pl.BlockSpec((tm * TOP_K,), lambda i: (jnp.minimum(i + 1, steps - 1),),
                               memory_space=pltpu.SMEM),
                  pl.BlockSpec((tm, TOP_K), row), pl.BlockSpec((tm, dw), row),
                  pl.BlockSpec((tm, d), row),
                  pl.BlockSpec((1,) + mods3.shape[1:], lambda i: (mod_row(i), 0, 0)),
                  pl.BlockSpec((1, d), const),
                  pl.BlockSpec((d, ds), const), pl.BlockSpec((d, ds), const),
                  pl.BlockSpec((ds, d), const),
                  pl.BlockSpec(memory_space=pl.ANY)],
        out_specs=(pl.BlockSpec((tm, d), lambda i: (jnp.minimum(i, lat_tiles - 1), 0)),
                   pl.BlockSpec((tm, d), lambda i: (jnp.maximum(i - lat_tiles, 0), 0))),
        scratch_shapes=[buf, buf, pltpu.SemaphoreType.DMA, pltpu.SemaphoreType.DMA],
        compiler_params=_params("arbitrary"),
        name="combine",
    )(pos_flat, pos_flat, w, h2u, x1, mods3, g_post, sg, su, sd, ys)


def _rope_tables(n_tokens):
    rows = n_tokens // GRID_W
    row = jnp.repeat(jnp.arange(rows, dtype=F32), GRID_W)
    col = jnp.tile(jnp.arange(GRID_W, dtype=F32), rows)
    n_freq = HEAD_DIM // 4
    freqs = ROPE_THETA ** (-jnp.arange(n_freq, dtype=F32) / n_freq)
    ang_r = row[:, None] * freqs
    ang_c = col[:, None] * freqs
    ang = jnp.concatenate([ang_r, ang_r, ang_c, ang_c], axis=-1)
    sign = jnp.where((jnp.arange(HEAD_DIM) % (HEAD_DIM // 2)) < HEAD_DIM // 4, -1.0, 1.0)
    return jnp.cos(ang), jnp.sin(ang) * sign


def _mixer(x3, mods3, mod_row, wts, rope_tabs, ctx, *, proj_dtype, tm, tq):
    b, n, d = x3.shape
    x = x3.reshape(b * n, d)
    qk_cols = 4 * N_HEADS * HEAD_DIM
    cols = {"cb": qk_cols + N_HEADS * V_HEAD_DIM}
    cols["cc"] = cols["cb"] + D_CONV
    cols["cx"] = cols["cc"] + D_CONV
    cols["ga"] = cols["cx"] + D_CONV
    cols["gc"] = cols["ga"] + d
    proj = _proj(x, mods3, wts["g_pre_mix"], wts["w_in"], mod_row=mod_row(PROJ_TILE[0]),
                 rope_tabs=rope_tabs, out_dtype=proj_dtype, gate_col=cols["ga"], rope_cols=qk_cols,
                 tm=PROJ_TILE[0], tn=PROJ_TILE[1])
    attn = _attention(proj.reshape(b, n, -1), wts["lam_vecs"], wts["g_subln"], ctx, tq=tq)
    merged = _merge(attn.reshape(b * n, -1), proj, wts["conv_w"], wts["w_attn_out"],
                    wts["w_conv_out"], seq_len=n, cols=cols, tm=tm)
    return merged, x, proj


def _moe(x1, h2u, logits, mods3, mod_row, wts, *, n_lat):
    t = x1.shape[0]
    idx, w, rank, counts = _route(logits, wts["router_bias"], tm=ROUTE_TILE)
    tiles_per = (counts[0] + EXPERT_TILE - 1) // EXPERT_TILE
    tile_end = jnp.cumsum(tiles_per)
    start = ((tile_end - tiles_per) * EXPERT_TILE)[None].astype(I32)
    n_tiles = t * TOP_K // EXPERT_TILE + N_EXPERTS
    tile_expert = jnp.minimum(
        jnp.sum(jnp.arange(n_tiles, dtype=I32)[:, None] >= tile_end[None, :], axis=1),
        N_EXPERTS - 1).astype(I32)
    n_valid = tile_end[-1:].astype(I32)
    last_row = ((tile_end - 1) * EXPERT_TILE).astype(I32)
    used = (tiles_per > 0).astype(I32)
    pos = _slots(idx, rank, start, tm=ROUTE_TILE).reshape(t * TOP_K)
    xs = _dispatch(last_row, used, n_valid, pos, h2u, n_tiles * EXPERT_TILE, tm=DISPATCH_TILE)
    ys = _experts(tile_expert, n_valid, xs, wts["w_exp_gate"], wts["w_exp_up"], wts["w_exp_down"])
    return _combine(pos, w, h2u, x1, mods3, wts["g_post_ffn"], wts["w_sh_gate"], wts["w_sh_up"],
                    wts["w_sh_down"], ys, mod_row=mod_row(COMBINE_TILE), tm=COMBINE_TILE,
                    n_lat=n_lat)


def kernel(x_prompt, x_sample, cache_k, cache_v, c, c_ctx, w_ada, b_ada, g_pre_mix, g_post_mix, w_in, lambda_q1, lambda_k1, lambda_q2, lambda_k2, g_subln, conv_w, w_attn_out, w_conv_out, w_o, g_pre_ffn, g_post_ffn, w_router, router_bias, w_exp_gate, w_exp_up, w_exp_down, w_sh_gate, w_sh_up, w_sh_down):
    bp, sp, d = x_prompt.shape
    bs, ss, _ = x_sample.shape
    past = cache_k.shape[2]
    l = 0
    wts = {
        "g_pre_mix": g_pre_mix[l][None], "g_post_mix": g_post_mix[l][None],
        "g_pre_ffn": g_pre_ffn[l][None], "g_post_ffn": g_post_ffn[l][None],
        "g_subln": g_subln[l][None], "conv_w": conv_w[l],
        "lam_vecs": jnp.stack([lambda_q1[l], lambda_k1[l], lambda_q2[l], lambda_k2[l]]),
        "router_bias": router_bias[l][None],
        "w_in": w_in[l].astype(BF16), "w_attn_out": w_attn_out[l].astype(BF16),
        "w_conv_out": w_conv_out[l].astype(BF16), "w_o": w_o[l].astype(BF16),
        "w_router": w_router[l].astype(BF16),
        "w_exp_gate": w_exp_gate[l], "w_exp_up": w_exp_up[l], "w_exp_down": w_exp_down[l],
        "w_sh_gate": w_sh_gate[l].astype(BF16), "w_sh_up": w_sh_up[l].astype(BF16),
        "w_sh_down": w_sh_down[l].astype(BF16),
    }
    pad_rows = 8 - (bs + 1)
    c_rows = jnp.concatenate([c, c_ctx[None], jnp.zeros((pad_rows, d), F32)], axis=0)
    mods3 = _ada(c_rows, w_ada[l], b_ada[l][None]).reshape(8, 6, d)

    tm = ROW_TILE
    ctx_row = lambda tm_: (lambda i: bs)
    lat_row = lambda tm_: (lambda i: (i * tm_) // ss)
    all_row = lambda tm_: (lambda i: jnp.minimum((i * tm_) // ss, bs))
    merged_p, xp, proj_p = _mixer(x_prompt, mods3, ctx_row, wts, None, None,
                                  proj_dtype=F32, tm=tm, tq=sp)
    hk = 2 * N_HEADS * HEAD_DIM
    new_k = proj_p[:, hk:2 * hk].reshape(bp, 1, sp, 2, N_HEADS, HEAD_DIM)
    new_v = proj_p[:, 2 * hk:2 * hk + N_HEADS * V_HEAD_DIM].reshape(bp, 1, sp, N_HEADS, V_HEAD_DIM)

    ctx = (cache_k[:, l].reshape(bs, past, hk).astype(BF16),
           cache_v[:, l].reshape(bs, past, N_HEADS * V_HEAD_DIM).astype(BF16))
    merged_s, xs, _ = _mixer(x_sample, mods3, lat_row, wts, _rope_tables(ss), ctx,
                             proj_dtype=BF16, tm=tm, tq=QUERY_TILE)

    n_lat = bs * ss
    x1, h2, logits = _post(merged_s, merged_p, xs, xp, mods3, wts["g_post_mix"], wts["g_pre_ffn"],
                           wts["w_o"], wts["w_router"], mod_row=all_row(POST_TILE), tm=POST_TILE)
    ys, yp = _moe(x1, h2, logits, mods3, all_row, wts, n_lat=n_lat)
    return (yp.reshape(bp, sp, d), ys.reshape(bs, ss, d), new_k, new_v)
```
